```python
import jax, jax.numpy as jnp
from jax import lax
import numpy as np

D_MODEL = 1024
BATCH = 16
SEQ = 4096
DEPTH = 4

HEAD_DIM = 64
N_HEADS = D_MODEL // HEAD_DIM
H_DIL = (N_HEADS * 3) // 8
H_FOX = (N_HEADS - H_DIL) // 2
H_SB = N_HEADS - H_DIL - H_FOX
DIL_PATTERNS = ((128, 1), (512, 4), (2048, 16))
BLOCK = 128
D_FF = ((8 * D_MODEL // 3 + 127) // 128) * 128
N_EXPERTS = 8
TOP_K = 2
N_DENSE = (DEPTH + 1) // 2
N_MOE = DEPTH // 2
PROJ_OUT = 3 * D_MODEL + H_FOX
EPS = 1e-6
NEG = -1e30

kernel_name = "hybrid_dilated_fox_stickbreaking_moe_block"


def _rmsnorm(x, g):
    xf = x.astype(jnp.float32)
    y = xf * lax.rsqrt(jnp.mean(xf * xf, axis=-1, keepdims=True) + EPS)
    return (y * g.astype(jnp.float32)).astype(x.dtype)


def _dilated_pattern(q, k, v, slopes, window, dilation):
    B, S, H, Dh = q.shape
    W = window // dilation
    n_prev = -(-W // BLOCK)
    L = S // dilation
    nb = -(-L // BLOCK)
    Lp = nb * BLOCK
    KB = (n_prev + 1) * BLOCK
    scale = HEAD_DIM ** -0.5

    def streams(t):
        return t.reshape(B, L, dilation, H, Dh)

    qs = jnp.pad(streams(q), ((0, 0), (0, Lp - L), (0, 0), (0, 0), (0, 0)))
    qs = qs.reshape(B, nb, BLOCK, dilation, H, Dh)
    pad_kv = ((0, 0), (n_prev * BLOCK, Lp - L), (0, 0), (0, 0), (0, 0))
    kb = jnp.pad(streams(k), pad_kv).reshape(B, nb + n_prev, BLOCK, dilation, H, Dh)
    vb = jnp.pad(streams(v), pad_kv).reshape(B, nb + n_prev, BLOCK, dilation, H, Dh)
    k_band = jnp.concatenate([kb[:, i:i + nb] for i in range(n_prev + 1)], axis=2)
    v_band = jnp.concatenate([vb[:, i:i + nb] for i in range(n_prev + 1)], axis=2)

    s = jnp.einsum('bnqrhd,bnkrhd->bnrhqk', qs, k_band).astype(jnp.float32) * scale
    qi = jnp.arange(BLOCK)[:, None]
    kj = jnp.arange(KB)[None, :]
    dist = qi - kj + n_prev * BLOCK
    blk = jnp.arange(nb)[:, None, None]
    valid = (dist >= 0) & (dist <= W) & (blk * BLOCK + kj >= n_prev * BLOCK)
    s = s - slopes[:, None, None] * (dist * dilation).astype(jnp.float32)
    s = jnp.where(valid[None, :, None, None], s, NEG)
    m = jnp.max(s, axis=-1, keepdims=True)
    p = jnp.exp(s - m)
    den = jnp.sum(p, axis=-1)
    o = jnp.einsum('bnrhqk,bnkrhd->bnqrhd', p.astype(v.dtype), v_band).astype(jnp.float32)
    o = o / jnp.transpose(den, (0, 1, 4, 2, 3))[..., None]
    lse = jnp.transpose(m[..., 0] + jnp.log(den), (0, 1, 4, 2, 3))
    o = o.reshape(B, Lp, dilation, H, Dh)[:, :L].reshape(B, S, H, Dh)
    lse = lse.reshape(B, Lp, dilation, H)[:, :L].reshape(B, S, H)
    return o, lse


def _dilated_attention(q, k, v):
    slopes = 2.0 ** (-8.0 * jnp.arange(1, H_DIL + 1, dtype=jnp.float32) / H_DIL)
    outs, lses = [], []
    for window, dilation in DIL_PATTERNS:
        o, lse = _dilated_pattern(q, k, v, slopes, window, dilation)
        outs.append(o)
        lses.append(lse)
    wts = jax.nn.softmax(jnp.stack(lses, axis=0), axis=0)
    return jnp.sum(wts[..., None] * jnp.stack(outs, axis=0), axis=0)


def _fox_attention(q, k, v, log_f_cum):
    B, S, H, Dh = q.shape
    nq = S // BLOCK
    scale = HEAD_DIM ** -0.5
    qb = q.reshape(B, nq, BLOCK, H, Dh).transpose(1, 0, 2, 3, 4)
    cb = log_f_cum.reshape(B, nq, BLOCK, H).transpose(1, 0, 3, 2)
    c_keys = jnp.transpose(log_f_cum, (0, 2, 1))
    kpos = jnp.arange(S)

    def block(args):
        qblk, cblk, start = args
        s = jnp.einsum('bqhd,bkhd->bhqk', qblk, k).astype(jnp.float32) * scale
        s = s + cblk[..., None] - c_keys[:, :, None, :]
        qpos = start + jnp.arange(BLOCK)
        s = jnp.where(kpos[None, :] <= qpos[:, None], s, NEG)
        p = jax.nn.softmax(s, axis=-1)
        return jnp.einsum('bhqk,bkhd->bqhd', p.astype(v.dtype), v).astype(jnp.float32)

    out = lax.map(block, (qb, cb, jnp.arange(nq) * BLOCK))
    return out.transpose(1, 0, 2, 3, 4).reshape(B, S, H, Dh)


def _stick_breaking_attention(q, k, v):
    B, S, H, Dh = q.shape
    nq = S // BLOCK
    scale = HEAD_DIM ** -0.5
    qb = q.reshape(B, nq, BLOCK, H, Dh).transpose(1, 0, 2, 3, 4)
    kpos = jnp.arange(S)

    def block(args):
        qblk, start = args
        z = jnp.einsum('bqhd,bkhd->bhqk', qblk, k).astype(jnp.float32) * scale
        qpos = start + jnp.arange(BLOCK)
        mask = kpos[None, :] < qpos[:, None]
        log_keep = jnp.where(mask, jax.nn.log_sigmoid(-z), 0.0)
        tail = lax.cumsum(log_keep, axis=3, reverse=True) - log_keep
        a = jnp.where(mask, jnp.exp(jax.nn.log_sigmoid(z) + tail), 0.0)
        return jnp.einsum('bhqk,bkhd->bqhd', a.astype(v.dtype), v).astype(jnp.float32)

    out = lax.map(block, (qb, jnp.arange(nq) * BLOCK))
    return out.transpose(1, 0, 2, 3, 4).reshape(B, S, H, Dh)


def _token_mixer(u, w_in, forget_bias, head_norm, w_out):
    B, S, D = u.shape
    proj = jnp.einsum('bsd,de->bse', u, w_in)
    q = proj[..., :D].reshape(B, S, N_HEADS, HEAD_DIM)
    k = proj[..., D:2 * D].reshape(B, S, N_HEADS, HEAD_DIM)
    v = proj[..., 2 * D:3 * D].reshape(B, S, N_HEADS, HEAD_DIM)
    f_logit = proj[..., 3 * D:].astype(jnp.float32) + forget_bias.astype(jnp.float32)
    log_f_cum = lax.cumsum(jax.nn.log_sigmoid(f_logit), axis=1)

    a, b = H_DIL, H_DIL + H_FOX
    o_dil = _dilated_attention(q[:, :, :a], k[:, :, :a], v[:, :, :a])
    o_fox = _fox_attention(q[:, :, a:b], k[:, :, a:b], v[:, :, a:b], log_f_cum)
    o_sb = _stick_breaking_attention(q[:, :, b:], k[:, :, b:], v[:, :, b:])
    o = jnp.concatenate([o_dil, o_fox, o_sb], axis=2)
    o = o * lax.rsqrt(jnp.mean(o * o, axis=-1, keepdims=True) + EPS)
    o = o * head_norm.astype(jnp.float32).reshape(N_HEADS, HEAD_DIM)
    o = o.astype(u.dtype).reshape(B, S, D)
    return jnp.einsum('bsd,de->bse', o, w_out)


def _swiglu(u, w_gate, w_up, w_down):
    hid = jax.nn.silu(jnp.einsum('bsd,df->bsf', u, w_gate)) * jnp.einsum('bsd,df->bsf', u, w_up)
    return jnp.einsum('bsf,fd->bsd', hid, w_down)


def _moe(u, router_w, w_gate, w_up, w_down):
    logits = jnp.einsum('bsd,de->bse', u, router_w).astype(jnp.float32)
    top_val, top_idx = lax.top_k(logits, TOP_K)
    gates = jax.nn.softmax(top_val, axis=-1)
    combine = jnp.sum(jax.nn.one_hot(top_idx, N_EXPERTS, dtype=jnp.float32) * gates[..., None], axis=-2)
    y = combine[..., 0:1] * _swiglu(u, w_gate[0], w_up[0], w_down[0]).astype(jnp.float32)
    for e in range(1, N_EXPERTS):
        y = y + combine[..., e:e + 1] * _swiglu(u, w_gate[e], w_up[e], w_down[e]).astype(jnp.float32)
    return y.astype(u.dtype)


def setup_inputs(seed: int = 0) -> dict:
    key = jax.random.key(seed)
    ks = jax.random.split(key, 16)
    f32 = jnp.float32
    sd = D_MODEL ** -0.5
    sf = D_FF ** -0.5
    return {
        "x": jax.random.normal(ks[0], (BATCH, SEQ, D_MODEL), f32),
        "mix_norm": 1.0 + 0.05 * jax.random.normal(ks[1], (DEPTH, D_MODEL), f32),
        "w_in": sd * jax.random.normal(ks[2], (DEPTH, D_MODEL, PROJ_OUT), f32),
        "forget_bias": 1.0 + 0.5 * jax.random.normal(ks[3], (DEPTH, H_FOX), f32),
        "head_norm": 1.0 + 0.05 * jax.random.normal(ks[4], (DEPTH, D_MODEL), f32),
        "w_out": sd * jax.random.normal(ks[5], (DEPTH, D_MODEL, D_MODEL), f32),
        "ffn_norm": 1.0 + 0.05 * jax.random.normal(ks[6], (DEPTH, D_MODEL), f32),
        "dense_w_gate": sd * jax.random.normal(ks[7], (N_DENSE, D_MODEL, D_FF), f32),
        "dense_w_up": sd * jax.random.normal(ks[8], (N_DENSE, D_MODEL, D_FF), f32),
        "dense_w_down": sf * jax.random.normal(ks[9], (N_DENSE, D_FF, D_MODEL), f32),
        "router_w": sd * jax.random.normal(ks[10], (N_MOE, D_MODEL, N_EXPERTS), f32),
        "moe_w_gate": sd * jax.random.normal(ks[11], (N_MOE, N_EXPERTS, D_MODEL, D_FF), f32),
        "moe_w_up": sd * jax.random.normal(ks[12], (N_MOE, N_EXPERTS, D_MODEL, D_FF), f32),
        "moe_w_down": sf * jax.random.normal(ks[13], (N_MOE, N_EXPERTS, D_FF, D_MODEL), f32),
        "final_norm": 1.0 + 0.05 * jax.random.normal(ks[14], (D_MODEL,), f32),
    }


def reference(x, mix_norm, w_in, forget_bias, head_norm, w_out, ffn_norm,
              dense_w_gate, dense_w_up, dense_w_down, router_w,
              moe_w_gate, moe_w_up, moe_w_down, final_norm):
    h = x
    for layer in range(DEPTH):
        u = _rmsnorm(h, mix_norm[layer])
        h = h + _token_mixer(u, w_in[layer], forget_bias[layer], head_norm[layer], w_out[layer])
        u = _rmsnorm(h, ffn_norm[layer])
        if layer % 2 == 0:
            i = layer // 2
            h = h + _swiglu(u, dense_w_gate[i], dense_w_up[i], dense_w_down[i])
        else:
            i = layer // 2
            h = h + _moe(u, router_w[i], moe_w_gate[i], moe_w_up[i], moe_w_down[i])
    return _rmsnorm(h, final_norm)
```

```python
import functools

import jax
import jax.numpy as jnp
from jax import lax
from jax.experimental import pallas as pl
from jax.experimental.pallas import tpu as pltpu

D_MODEL = 1024
HEAD_DIM = 64
H_DIL, H_FOX, H_SB = 6, 5, 5
DIL_PATTERNS = ((128, 1), (512, 4), (2048, 16))
BAND = 128
D_FF = 2816
N_EXPERTS = 8
EPS = 1e-6
NEG = -1e30
LANES = 128
W_DIL = H_DIL * HEAD_DIM
W_FS = 3 * (H_FOX + H_SB) * HEAD_DIM
W_O = 384
VMEM_LIMIT = 56 * 1024 * 1024

F32 = jnp.float32
BF16 = jnp.bfloat16
NT = (((1,), (1,)), ((), ()))


def _dot(a, b):
    return jnp.dot(a, b, preferred_element_type=F32)


def _dot_nt(a, b):
    return lax.dot_general(a, b, NT, preferred_element_type=F32)


def _rms(x, g):
    return x * lax.rsqrt(jnp.mean(x * x, axis=-1, keepdims=True) + EPS) * g


def _split3(x):
    x1 = x.astype(BF16)
    r = x - x1.astype(F32)
    x2 = r.astype(BF16)
    x3 = (r - x2.astype(F32)).astype(BF16)
    return x1, x2, x3


def _cparams(sem):
    return pltpu.CompilerParams(dimension_semantics=sem, vmem_limit_bytes=VMEM_LIMIT)


def _inproj_kernel(x_ref, g_ref, wd_ref, wf_ref, wl_ref, fb_ref, dil_ref, fs_ref, lf_ref, u_ref):
    u_ref[...] = _rms(x_ref[...], g_ref[...]).astype(BF16)
    u = u_ref[...]
    for w_ref, o_ref in ((wd_ref, dil_ref), (wf_ref, fs_ref)):
        n = w_ref.shape[1]
        for c in range(0, n, 512):
            e = min(c + 512, n)
            o_ref[:, c:e] = _dot(u, w_ref[:, c:e]).astype(BF16)
    z = _dot(u, wl_ref[...]) + fb_ref[...]
    lf_ref[...] = jnp.minimum(z, 0.0) - jnp.log1p(jnp.exp(-jnp.abs(z)))


def _inproj(h, g, wd, wf, wl, fb, tm):
    t = h.shape[0]
    full = lambda a: pl.BlockSpec(a.shape, lambda i: (0, 0))
    return pl.pallas_call(
        _inproj_kernel,
        grid=(t // tm,),
        in_specs=[pl.BlockSpec((tm, D_MODEL), lambda i: (i, 0)), full(g), full(wd), full(wf), full(wl), full(fb)],
        out_specs=[pl.BlockSpec((tm, 3 * W_DIL), lambda i: (i, 0)),
                   pl.BlockSpec((tm, W_FS), lambda i: (i, 0)),
                   pl.BlockSpec((tm, LANES), lambda i: (i, 0))],
        out_shape=[jax.ShapeDtypeStruct((t, 3 * W_DIL), BF16),
                   jax.ShapeDtypeStruct((t, W_FS), BF16),
                   jax.ShapeDtypeStruct((t, LANES), F32)],
        scratch_shapes=[pltpu.VMEM((tm, D_MODEL), BF16)],
        compiler_params=_cparams(("parallel",)),
        name="inproj",
    )(h, g, wd, wf, wl, fb)


def _prep_kernel(fs_ref, lf_ref, q_ref, k_ref, vt_ref, carry_ref, *, tc, tk):
    @pl.when(pl.program_id(1) == 0)
    def _():
        carry_ref[...] = jnp.zeros_like(carry_ref)

    row = lax.broadcasted_iota(jnp.int32, (tc, tc), 0)
    col = lax.broadcasted_iota(jnp.int32, (tc, tc), 1)
    tri = jnp.where(col <= row, 1.0, 0.0).astype(BF16)
    x1, x2, x3 = _split3(lf_ref[...])
    c = _dot(tri, x1) + _dot(tri, x2) + _dot(tri, x3) + carry_ref[0:1, :]
    carry_ref[0:1, :] = c[tc - 1:tc, :]

    n_heads = H_FOX + H_SB
    for s in range(n_heads // 2):
        vt = fs_ref[:, LANES * s:LANES * (s + 1)].astype(F32).T
        for half in range(2):
            for ci in range(tc // tk):
                vt_ref[2 * s + half, ci] = vt[HEAD_DIM * half:HEAD_DIM * (half + 1), ci * tk:(ci + 1) * tk].astype(BF16)

    lane = lax.broadcasted_iota(jnp.int32, (tc, LANES), 1)
    low = lane < HEAD_DIM

    def head(col0):
        s = col0 // LANES
        x = fs_ref[:, LANES * s:LANES * (s + 1)].astype(F32)
        if (col0 // HEAD_DIM) % 2 == 1:
            x = pltpu.roll(x, HEAD_DIM, 1)
        return jnp.where(low, x, 0.0)

    base = n_heads * HEAD_DIM
    for j in range(H_FOX):
        cj = c[:, j:j + 1]
        c1 = cj.astype(BF16).astype(F32)
        r = cj - c1
        c2 = r.astype(BF16).astype(F32)
        c3 = r - c2
        qa = head(base + HEAD_DIM * j)
        ka = head(base + HEAD_DIM * (H_FOX + j))
        for off, (qv, kv) in enumerate(((c1, 1.0), (c2, 1.0), (c3, 1.0), (1.0, -c1), (1.0, -c2), (1.0, -c3))):
            sel = lane == HEAD_DIM + off
            qa = jnp.where(sel, qv, qa)
            ka = jnp.where(sel, kv, ka)
        q_ref[j] = qa.astype(BF16)
        k_ref[j] = ka.astype(BF16)
    base_sb = base + 2 * H_FOX * HEAD_DIM
    for j in range(H_SB):
        q_ref[H_FOX + j] = head(base_sb + HEAD_DIM * j).astype(BF16)
        k_ref[H_FOX + j] = head(base_sb + HEAD_DIM * (H_SB + j)).astype(BF16)


def _prep(fs, lf, b, s, tc, tk):
    nh = H_FOX + H_SB
    nk = s // tk
    return pl.pallas_call(
        functools.partial(_prep_kernel, tc=tc, tk=tk),
        grid=(b, s // tc),
        in_specs=[pl.BlockSpec((None, tc, W_FS), lambda bi, si: (bi, si, 0)),
                  pl.BlockSpec((None, tc, LANES), lambda bi, si: (bi, si, 0))],
        out_specs=[pl.BlockSpec((None, nh, tc, LANES), lambda bi, si: (bi, 0, si, 0)),
                   pl.BlockSpec((None, nh, tc, LANES), lambda bi, si: (bi, 0, si, 0)),
                   pl.BlockSpec((None, nh, tc // tk, HEAD_DIM, tk), lambda bi, si: (bi, 0, si, 0, 0))],
        out_shape=[jax.ShapeDtypeStruct((b, nh, s, LANES), BF16),
                   jax.ShapeDtypeStruct((b, nh, s, LANES), BF16),
                   jax.ShapeDtypeStruct((b, nh, nk, HEAD_DIM, tk), BF16)],
        scratch_shapes=[pltpu.VMEM((8, LANES), F32)],
        compiler_params=_cparams(("parallel", "arbitrary")),
        name="prep",
    )(fs.reshape(b, s, W_FS), lf.reshape(b, s, LANES))


def _attn_kernel(*refs, kind, tq, tk):
    if kind == "sb":
        q0, q1, k0, k1, v0, v1, g_ref, up_ref, o_ref, o2t_ref = refs
    else:
        q0, q1, k0, k1, v0, v1, g_ref, o_ref, o2t_ref = refs
    p = pl.program_id(1)
    qi = pl.program_id(2)
    row = lax.broadcasted_iota(jnp.int32, (tk, tq), 0)
    col = lax.broadcasted_iota(jnp.int32, (tk, tq), 1)

    def kv(k_ref, v_ref, j):
        return k_ref[pl.ds(pl.multiple_of(j * tk, tk), tk), :], v_ref[j]

    def fox_head(q_ref, k_ref, v_ref):
        q = q_ref[...]

        def step(j, carry, diag):
            m, l, acc = carry
            kb, vt = kv(k_ref, v_ref, j)
            st = _dot_nt(kb, q)
            if diag:
                st = jnp.where(row <= col, st, NEG)
            m_new = jnp.maximum(m, jnp.max(st, axis=0, keepdims=True))
            alpha = jnp.exp(m - m_new)
            pt = jnp.exp(st - m_new)
            l = alpha * l + jnp.sum(pt, axis=0, keepdims=True)
            acc = alpha * acc + _dot(vt, pt.astype(BF16))
            return m_new, l, acc

        init = (jnp.full((1, tq), NEG, F32), jnp.zeros((1, tq), F32), jnp.zeros((HEAD_DIM, tq), F32))
        carry = lax.fori_loop(0, qi, lambda j, c: step(j, c, False), init)
        _, l, acc = step(qi, carry, True)
        return acc / l

    def sb_head(q_ref, k_ref, v_ref):
        q = q_ref[...]
        up = up_ref[...]

        def step(j, carry, diag):
            r, acc = carry
            kb, vt = kv(k_ref, v_ref, j)
            z = _dot_nt(kb, q)
            sp = jnp.maximum(z, 0.0) + jnp.log1p(jnp.exp(-jnp.abs(z)))
            lk = -sp
            if diag:
                lk = jnp.where(row < col, lk, 0.0)
            hi = lk.astype(BF16)
            lo = (lk - hi.astype(F32)).astype(BF16)
            tail = _dot(up, hi) + _dot(up, lo) + r
            a = jnp.exp(z - sp + tail)
            if diag:
                a = jnp.where(row < col, a, 0.0)
            acc = acc + _dot(vt, a.astype(BF16))
            r = r + jnp.sum(lk, axis=0, keepdims=True)
            return r, acc

        carry = step(qi, (jnp.zeros((1, tq), F32), jnp.zeros((HEAD_DIM, tq), F32)), True)
        _, acc = lax.fori_loop(0, qi, lambda i, c: step(qi - 1 - i, c, False), carry)
        return acc

    head = sb_head if kind == "sb" else fox_head

    def normed(q_ref, k_ref, v_ref, slot):
        o = head(q_ref, k_ref, v_ref)
        ms = jnp.mean(o * o, axis=0, keepdims=True)
        return o * lax.rsqrt(ms + EPS) * g_ref[2 * p + slot]

    o2t_ref[0:HEAD_DIM, :] = normed(q0, k0, v0, 0)

    @pl.when(p < 2)
    def _():
        o2t_ref[HEAD_DIM:, :] = normed(q1, k1, v1, 1)

    @pl.when(p == 2)
    def _():
        o2t_ref[HEAD_DIM:, :] = jnp.zeros((HEAD_DIM, tq), F32)

    o_ref[...] = o2t_ref[...].T.astype(BF16)


def _attn(q, k, vt, gains, kind, tq, tk):
    b, _, s, _ = q.shape
    nk = s // tk
    base = H_FOX if kind == "sb" else 0
    last = H_FOX - 1
    h0 = lambda p: base + 2 * p
    h1 = lambda p: base + jnp.minimum(2 * p + 1, last)
    qspec = lambda hf: pl.BlockSpec((None, None, tq, LANES), lambda bi, p, qi: (bi, hf(p), qi, 0))
    kspec = lambda hf: pl.BlockSpec((None, None, s, LANES), lambda bi, p, qi: (bi, hf(p), 0, 0))
    vspec = lambda hf: pl.BlockSpec((None, None, nk, HEAD_DIM, tk), lambda bi, p, qi: (bi, hf(p), 0, 0, 0))
    in_specs = [qspec(h0), qspec(h1), kspec(h0), kspec(h1), vspec(h0), vspec(h1),
                pl.BlockSpec(gains.shape, lambda bi, p, qi: (0, 0, 0))]
    args = [q, q, k, k, vt, vt, gains]
    if kind == "sb":
        upper = jnp.triu(jnp.ones((tk, tk), F32), k=1).astype(BF16)
        in_specs.append(pl.BlockSpec((tk, tk), lambda bi, p, qi: (0, 0)))
        args.append(upper)
    return pl.pallas_call(
        functools.partial(_attn_kernel, kind=kind, tq=tq, tk=tk),
        grid=(b, 3, s // tq),
        in_specs=in_specs,
        out_specs=pl.BlockSpec((None, tq, LANES), lambda bi, p, qi: (bi, qi, p)),
        out_shape=jax.ShapeDtypeStruct((b, s, W_O), BF16),
        scratch_shapes=[pltpu.VMEM((2 * HEAD_DIM, tq), F32)],
        compiler_params=_cparams(("parallel", "parallel", "arbitrary")),
        name=kind + "_attn",
    )(*args)


def _dilated_kernel(*refs, tl, has_prev, is_last):
    it = iter(refs)
    q_ref, km_ref, kh_ref, vm_ref, vh_ref, bm_ref = (next(it) for _ in range(6))
    po_ref = pl_ref = g_ref = None
    if has_prev:
        po_ref, pl_ref = next(it), next(it)
    if is_last:
        g_ref = next(it)
    o_ref = next(it)
    lse_ref = None if is_last else next(it)

    first = pl.program_id(2) == 0
    lane = lax.broadcasted_iota(jnp.int32, (1, LANES), 1)
    half_mask = [(lane < HEAD_DIM).astype(F32), (lane >= HEAD_DIM).astype(F32)]
    lane_q = lax.broadcasted_iota(jnp.int32, (BAND, LANES), 1)
    colk = lax.broadcasted_iota(jnp.int32, (BAND, 2 * BAND), 1)

    kf = jnp.concatenate([kh_ref[...], km_ref[...]], axis=0).astype(F32)
    vf = jnp.concatenate([vh_ref[...], vm_ref[...]], axis=0).astype(F32)

    for jj in range(tl // BAND):
        rows = slice(jj * BAND, (jj + 1) * BAND)
        krows = slice(jj * BAND, jj * BAND + 2 * BAND)
        lse_tile = jnp.zeros((BAND, LANES), F32)
        for s in range(H_DIL // 2):
            lanes = slice(LANES * s, LANES * (s + 1))
            qb = q_ref[rows, lanes]
            o_slab = jnp.zeros((BAND, LANES), F32)
            for par in range(2):
                h = 2 * s + par
                kk = (kf[krows, lanes] * half_mask[par]).astype(BF16)
                vv = (vf[krows, lanes] * half_mask[par]).astype(BF16)
                sc = _dot_nt(qb, kk) + bm_ref[h]
                if jj == 0:
                    sc = jnp.where(jnp.logical_and(first, colk < BAND), NEG, sc)
                m = jnp.max(sc, axis=-1, keepdims=True)
                pe = jnp.exp(sc - m)
                den = jnp.sum(pe, axis=-1, keepdims=True)
                o = _dot(pe.astype(BF16), vv) / den
                lse = m + jnp.log(den)
                if has_prev:
                    lp = pl_ref[rows, :][:, h:h + 1]
                    mm = jnp.maximum(lp, lse)
                    e1 = jnp.exp(lp - mm)
                    e2 = jnp.exp(lse - mm)
                    o = (e1 * (po_ref[rows, lanes] * half_mask[par]) + e2 * o) / (e1 + e2)
                    lse = mm + jnp.log(e1 + e2)
                if is_last:
                    ms = jnp.sum(o * o, axis=-1, keepdims=True) * (1.0 / HEAD_DIM)
                    o = o * lax.rsqrt(ms + EPS)
                else:
                    lse_tile = jnp.where(lane_q == h, lse, lse_tile)
                o_slab = o_slab + o
            if is_last:
                o_ref[rows, lanes] = (o_slab * g_ref[:, lanes]).astype(o_ref.dtype)
            else:
                o_ref[rows, lanes] = o_slab
        if not is_last:
            lse_ref[rows, :] = lse_tile


def _dilated(dil, prev, bm, gains, b, s, d, is_last):
    l = s // d
    tl = min(512, l)
    nsub = tl // BAND
    has_prev = prev is not None
    view = dil.reshape(b, l, d * 3 * W_DIL)
    main = lambda part: pl.BlockSpec((None, tl, W_DIL), lambda bi, r, i: (bi, i, 3 * r + part))
    halo = lambda part: pl.BlockSpec((None, BAND, W_DIL),
                                     lambda bi, r, i: (bi, jnp.maximum(i * nsub - 1, 0), 3 * r + part))
    ospec = pl.BlockSpec((None, tl, W_DIL), lambda bi, r, i: (bi, i, r))
    lspec = pl.BlockSpec((None, tl, LANES), lambda bi, r, i: (bi, i, r))
    in_specs = [main(0), main(1), halo(1), main(2), halo(2), pl.BlockSpec(bm.shape, lambda bi, r, i: (0, 0, 0))]
    args = [view, view, view, view, view, bm]
    if has_prev:
        in_specs += [ospec, lspec]
        args += [prev[0].reshape(b, l, d * W_DIL), prev[1].reshape(b, l, d * LANES)]
    if is_last:
        in_specs.append(pl.BlockSpec(gains.shape, lambda bi, r, i: (0, 0)))
        args.append(gains)
        out_specs = ospec
        out_shape = jax.ShapeDtypeStruct((b, l, d * W_DIL), BF16)
    else:
        out_specs = [ospec, lspec]
        out_shape = [jax.ShapeDtypeStruct((b, l, d * W_DIL), F32), jax.ShapeDtypeStruct((b, l, d * LANES), F32)]
    out = pl.pallas_call(
        functools.partial(_dilated_kernel, tl=tl, has_prev=has_prev, is_last=is_last),
        grid=(b, d, l // tl),
        in_specs=in_specs,
        out_specs=out_specs,
        out_shape=out_shape,
        compiler_params=_cparams(("parallel", "parallel", "arbitrary")),
        name=f"dilated_d{d}",
    )(*args)
    if is_last:
        return out.reshape(b * s, W_DIL)
    return out[0].reshape(b * s, W_DIL), out[1].reshape(b * s, LANES)


def _alibi_bias(d):
    slopes = 2.0 ** (-8.0 * jnp.arange(1, H_DIL + 1, dtype=F32) / H_DIL)
    dist = jnp.arange(BAND)[:, None] - jnp.arange(2 * BAND)[None, :] + BAND
    valid = (dist >= 0) & (dist <= BAND)
    bias = -slopes[:, None, None] * (dist * d).astype(F32)[None]
    return jnp.where(valid[None], bias, NEG)


def _outproj_kernel(h_ref, od_ref, of_ref, os_ref, wd_ref, wf_ref, ws_ref, o_ref):
    o_ref[...] = (h_ref[...] + _dot(od_ref[...], wd_ref[...]) + _dot(of_ref[...], wf_ref[...])
                  + _dot(os_ref[...], ws_ref[...]))


def _outproj(h, od, of, os_, wd, wf, ws, tm):
    t = h.shape[0]
    full = lambda a: pl.BlockSpec(a.shape, lambda i: (0, 0))
    tile = lambda w: pl.BlockSpec((tm, w), lambda i: (i, 0))
    return pl.pallas_call(
        _outproj_kernel,
        grid=(t // tm,),
        in_specs=[tile(D_MODEL), tile(W_DIL), tile(W_O), tile(W_O), full(wd), full(wf), full(ws)],
        out_specs=tile(D_MODEL),
        out_shape=jax.ShapeDtypeStruct((t, D_MODEL), F32),
        compiler_params=_cparams(("parallel",)),
        name="outproj",
    )(h, od, of, os_, wd, wf, ws)


def _swiglu_tile(u, wg_ref, wu_ref, wd_ref, hid_ref):
    for c in range(0, D_FF, 256):
        g = _dot(u, wg_ref[:, c:c + 256])
        v = _dot(u, wu_ref[:, c:c + 256])
        hid_ref[:, c:c + 256] = (g * (1.0 / (1.0 + jnp.exp(-g))) * v).astype(BF16)
    return _dot(hid_ref[...], wd_ref[...])


def _ffn_kernel(x_ref, g_ref, wg_ref, wu_ref, wd_ref, o_ref, u_ref, hid_ref):
    x = x_ref[...]
    u_ref[...] = _rms(x, g_ref[...]).astype(BF16)
    o_ref[...] = x + _swiglu_tile(u_ref[...], wg_ref, wu_ref, wd_ref, hid_ref)


def _ffn(h, g, wg, wu, wd, tm):
    t = h.shape[0]
    full = lambda a: pl.BlockSpec(a.shape, lambda i: (0, 0))
    tile = pl.BlockSpec((tm, D_MODEL), lambda i: (i, 0))
    return pl.pallas_call(
        _ffn_kernel,
        grid=(t // tm,),
        in_specs=[tile, full(g), full(wg), full(wu), full(wd)],
        out_specs=tile,
        out_shape=jax.ShapeDtypeStruct((t, D_MODEL), F32),
        scratch_shapes=[pltpu.VMEM((tm, D_MODEL), BF16), pltpu.VMEM((tm, D_FF), BF16)],
        compiler_params=_cparams(("parallel",)),
        name="ffn",
    )(h, g, wg, wu, wd)


def _moe_kernel(x_ref, g_ref, rw_ref, wg_ref, wu_ref, wd_ref, fg_ref, o_ref, u_ref, hid_ref, comb_ref, acc_ref,
                *, final_norm):
    e = pl.program_id(1)
    lane = lax.broadcasted_iota(jnp.int32, comb_ref.shape, 1)

    @pl.when(e == 0)
    def _():
        u = _rms(x_ref[...], g_ref[...])
        u_ref[...] = u.astype(BF16)
        logits = jnp.dot(u, rw_ref[...], preferred_element_type=F32, precision=lax.Precision.HIGHEST)
        logits = jnp.where(lane < N_EXPERTS, logits, NEG)
        t1 = jnp.max(logits, axis=-1, keepdims=True)
        i1 = jnp.min(jnp.where(logits == t1, lane, LANES), axis=-1, keepdims=True)
        rest = jnp.where(lane == i1, NEG, logits)
        t2 = jnp.max(rest, axis=-1, keepdims=True)
        i2 = jnp.min(jnp.where(rest == t2, lane, LANES), axis=-1, keepdims=True)
        e2 = jnp.exp(t2 - t1)
        comb_ref[...] = jnp.where(lane == i1, 1.0 / (1.0 + e2), 0.0) + jnp.where(lane == i2, e2 / (1.0 + e2), 0.0)
        acc_ref[...] = jnp.zeros_like(acc_ref)

    y = _swiglu_tile(u_ref[...], wg_ref, wu_ref, wd_ref, hid_ref)
    w = jnp.sum(jnp.where(lane == e, comb_ref[...], 0.0), axis=-1, keepdims=True)
    acc_ref[...] += w * y

    @pl.when(e == N_EXPERTS - 1)
    def _():
        res = x_ref[...] + acc_ref[...]
        if final_norm:
            res = _rms(res, fg_ref[...])
        o_ref[...] = res


def _moe(h, g, rw, wg, wu, wd, fg, tm, final_norm):
    t = h.shape[0]
    full = lambda a: pl.BlockSpec(a.shape, lambda i, e: (0, 0))
    tile = pl.BlockSpec((tm, D_MODEL), lambda i, e: (i, 0))
    expert = lambda a: pl.BlockSpec((None,) + a.shape[1:], lambda i, e: (e, 0, 0))
    return pl.pallas_call(
        functools.partial(_moe_kernel, final_norm=final_norm),
        grid=(t // tm, N_EXPERTS),
        in_specs=[tile, full(g), full(rw), expert(wg), expert(wu), expert(wd), full(fg)],
        out_specs=tile,
        out_shape=jax.ShapeDtypeStruct((t, D_MODEL), F32),
        scratch_shapes=[pltpu.VMEM((tm, D_MODEL), BF16), pltpu.VMEM((tm, D_FF), BF16),
                        pltpu.VMEM((tm, LANES), F32), pltpu.VMEM((tm, D_MODEL), F32)],
        compiler_params=_cparams(("parallel", "arbitrary")),
        name="moe",
    )(h, g, rw, wg, wu, wd, fg)


def _rmsnorm_kernel(x_ref, g_ref, o_ref):
    o_ref[...] = _rms(x_ref[...], g_ref[...])


def _rmsnorm(h, g, tm):
    t = h.shape[0]
    tile = pl.BlockSpec((tm, D_MODEL), lambda i: (i, 0))
    return pl.pallas_call(
        _rmsnorm_kernel, grid=(t // tm,),
        in_specs=[tile, pl.BlockSpec(g.shape, lambda i: (0, 0))], out_specs=tile,
        out_shape=jax.ShapeDtypeStruct((t, D_MODEL), F32),
        compiler_params=_cparams(("parallel",)), name="final_norm",
    )(h, g)


def _mixer_weights(w_in, forget_bias, head_norm, w_out, tq):
    d = D_MODEL
    a, b = W_DIL, W_DIL + H_FOX * HEAD_DIM
    wq, wk, wv, wl = w_in[:, :d] * 0.125, w_in[:, d:2 * d], w_in[:, 2 * d:3 * d], w_in[:, 3 * d:]
    wd = jnp.concatenate([wq[:, :a], wk[:, :a], wv[:, :a]], axis=1).astype(BF16)
    wf = jnp.concatenate([wv[:, a:b], wv[:, b:], wq[:, a:b], wk[:, a:b], wq[:, b:], wk[:, b:]], axis=1).astype(BF16)
    wl = jnp.pad(wl, ((0, 0), (0, LANES - H_FOX))).astype(BF16)
    fb = jnp.pad(forget_bias, (0, LANES - H_FOX)).reshape(1, LANES)
    pad_rows = jnp.zeros((HEAD_DIM, d), F32)
    wo_d = w_out[:a].astype(BF16)
    wo_f = jnp.concatenate([w_out[a:b], pad_rows], axis=0).astype(BF16)
    wo_s = jnp.concatenate([w_out[b:], pad_rows], axis=0).astype(BF16)
    pad_gain = jnp.ones((HEAD_DIM,), F32)
    gcol = lambda g: jnp.broadcast_to(jnp.concatenate([g, pad_gain]).reshape(6, HEAD_DIM, 1), (6, HEAD_DIM, tq))
    return dict(wd=wd, wf=wf, wl=wl, fb=fb, wo_d=wo_d, wo_f=wo_f, wo_s=wo_s,
                g_dil=head_norm[:a].reshape(1, a), g_fox=gcol(head_norm[a:b]), g_sb=gcol(head_norm[b:]))


def _token_mixer(h, norm_g, w, bias_mats, b, s, tm, tq, tk):
    dil, fs, lf = _inproj(h, norm_g, w["wd"], w["wf"], w["wl"], w["fb"], tm)
    q, k, vt = _prep(fs, lf, b, s, min(512, s), tk)
    prev = None
    for pi, (_, d) in enumerate(DIL_PATTERNS):
        is_last = pi == len(DIL_PATTERNS) - 1
        prev = _dilated(dil, prev, bias_mats[pi], w["g_dil"], b, s, d, is_last)
    o_dil = prev
    o_fox = _attn(q, k, vt, w["g_fox"], "fox", tq, tk).reshape(b * s, W_O)
    o_sb = _attn(q, k, vt, w["g_sb"], "sb", tq, tk).reshape(b * s, W_O)
    return _outproj(h, o_dil, o_fox, o_sb, w["wo_d"], w["wo_f"], w["wo_s"], tm)


def kernel(x, mix_norm, w_in, forget_bias, head_norm, w_out, ffn_norm, dense_w_gate, dense_w_up, dense_w_down,
           router_w, moe_w_gate, moe_w_up, moe_w_down, final_norm):
    b, s, d = x.shape
    depth = mix_norm.shape[0]
    tm, tq, tk = 512, 256, 256
    h = x.reshape(b * s, d)
    bias_mats = [_alibi_bias(dd) for _, dd in DIL_PATTERNS]
    fg = final_norm.reshape(1, d)
    for layer in range(depth):
        w = _mixer_weights(w_in[layer], forget_bias[layer], head_norm[layer], w_out[layer], tq)
        h = _token_mixer(h, mix_norm[layer].reshape(1, d), w, bias_mats, b, s, tm, tq, tk)
        g = ffn_norm[layer].reshape(1, d)
        i = layer // 2
        if layer % 2 == 0:
            h = _ffn(h, g, dense_w_gate[i].astype(BF16), dense_w_up[i].astype(BF16), dense_w_down[i].astype(BF16), tm)
        else:
            rw = jnp.pad(router_w[i], ((0, 0), (0, LANES - N_EXPERTS)))
            h = _moe(h, g, rw, moe_w_gate[i].astype(BF16), moe_w_up[i].astype(BF16), moe_w_down[i].astype(BF16),
                     fg, tm, final_norm=(layer == depth - 1))
    if depth % 2 == 1:
        h = _rmsnorm(h, fg, tm)
    return h.reshape(b, s, d)
```

```python
import functools

import jax
import jax.numpy as jnp
from jax import lax
from jax.experimental import pallas as pl
from jax.experimental.pallas import tpu as pltpu

D_MODEL = 1024
HEAD_DIM = 64
H_DIL, H_FOX, H_SB = 6, 5, 5
DIL_PATTERNS = ((128, 1), (512, 4), (2048, 16))
BAND = 128
D_FF = 2816
N_EXPERTS = 8
EPS = 1e-6
NEG = -1e30
LANES = 128
W_DIL = H_DIL * HEAD_DIM
W_FOX = H_FOX * HEAD_DIM
W_FS = 3 * (H_FOX + H_SB) * HEAD_DIM
VMEM_LIMIT = 56 * 1024 * 1024

F32 = jnp.float32
BF16 = jnp.bfloat16
NT = (((1,), (1,)), ((), ()))
TN = (((0,), (0,)), ((), ()))


def _dot(a, b):
    return jnp.dot(a, b, preferred_element_type=F32)


def _dot_nt(a, b):
    return lax.dot_general(a, b, NT, preferred_element_type=F32)


def _dot_tn(a, b):
    return lax.dot_general(a, b, TN, preferred_element_type=F32)


def _rms(x, g):
    return x * lax.rsqrt(jnp.mean(x * x, axis=-1, keepdims=True) + EPS) * g


def _split3(x):
    x1 = x.astype(BF16)
    r = x - x1.astype(F32)
    x2 = r.astype(BF16)
    x3 = (r - x2.astype(F32)).astype(BF16)
    return x1, x2, x3


def _cparams(sem):
    return pltpu.CompilerParams(dimension_semantics=sem, vmem_limit_bytes=VMEM_LIMIT)


def _inproj_kernel(x_ref, g_ref, wd_ref, wf_ref, wl_ref, fb_ref, dil_ref, fs_ref, lf_ref, u_ref):
    u_ref[...] = _rms(x_ref[...], g_ref[...]).astype(BF16)
    u = u_ref[...]
    for w_ref, o_ref in ((wd_ref, dil_ref), (wf_ref, fs_ref)):
        n = w_ref.shape[1]
        for c in range(0, n, 512):
            e = min(c + 512, n)
            o_ref[:, c:e] = _dot(u, w_ref[:, c:e]).astype(BF16)
    z = _dot(u, wl_ref[...]) + fb_ref[...]
    lf_ref[...] = jnp.minimum(z, 0.0) - jnp.log1p(jnp.exp(-jnp.abs(z)))


def _inproj(h, g, wd, wf, wl, fb, tm):
    t = h.shape[0]
    full = lambda a: pl.BlockSpec(a.shape, lambda i: (0, 0))
    return pl.pallas_call(
        _inproj_kernel,
        grid=(t // tm,),
        in_specs=[pl.BlockSpec((tm, D_MODEL), lambda i: (i, 0)), full(g), full(wd), full(wf), full(wl), full(fb)],
        out_specs=[pl.BlockSpec((tm, 3 * W_DIL), lambda i: (i, 0)),
                   pl.BlockSpec((tm, W_FS), lambda i: (i, 0)),
                   pl.BlockSpec((tm, LANES), lambda i: (i, 0))],
        out_shape=[jax.ShapeDtypeStruct((t, 3 * W_DIL), BF16),
                   jax.ShapeDtypeStruct((t, W_FS), BF16),
                   jax.ShapeDtypeStruct((t, LANES), F32)],
        scratch_shapes=[pltpu.VMEM((tm, D_MODEL), BF16)],
        compiler_params=_cparams(("parallel",)),
        name="inproj",
    )(h, g, wd, wf, wl, fb)


def _prep_kernel(fs_ref, lf_ref, q_ref, k_ref, vt_ref, carry_ref, *, tc, tk):
    @pl.when(pl.program_id(1) == 0)
    def _():
        carry_ref[...] = jnp.zeros_like(carry_ref)

    row = lax.broadcasted_iota(jnp.int32, (tc, tc), 0)
    col = lax.broadcasted_iota(jnp.int32, (tc, tc), 1)
    tri = jnp.where(col <= row, 1.0, 0.0).astype(BF16)
    x1, x2, x3 = _split3(lf_ref[...])
    c = _dot(tri, x1) + _dot(tri, x2) + _dot(tri, x3) + carry_ref[0:1, :]
    carry_ref[0:1, :] = c[tc - 1:tc, :]

    n_heads = H_FOX + H_SB
    for s in range(n_heads // 2):
        vt = fs_ref[:, LANES * s:LANES * (s + 1)].astype(F32).T
        for half in range(2):
            for ci in range(tc // tk):
                vt_ref[2 * s + half, ci] = vt[HEAD_DIM * half:HEAD_DIM * (half + 1), ci * tk:(ci + 1) * tk].astype(BF16)

    lane = lax.broadcasted_iota(jnp.int32, (tc, LANES), 1)
    low = lane < HEAD_DIM

    def head(col0):
        s = col0 // LANES
        x = fs_ref[:, LANES * s:LANES * (s + 1)].astype(F32)
        if (col0 // HEAD_DIM) % 2 == 1:
            x = pltpu.roll(x, HEAD_DIM, 1)
        return jnp.where(low, x, 0.0)

    base = n_heads * HEAD_DIM
    for j in range(H_FOX):
        cj = c[:, j:j + 1]
        c1 = cj.astype(BF16).astype(F32)
        r = cj - c1
        c2 = r.astype(BF16).astype(F32)
        c3 = r - c2
        qa = head(base + HEAD_DIM * j)
        ka = head(base + HEAD_DIM * (H_FOX + j))
        for off, (qv, kv) in enumerate(((c1, 1.0), (c2, 1.0), (c3, 1.0), (1.0, -c1), (1.0, -c2), (1.0, -c3))):
            sel = lane == HEAD_DIM + off
            qa = jnp.where(sel, qv, qa)
            ka = jnp.where(sel, kv, ka)
        q_ref[j] = qa.astype(BF16)
        k_ref[j] = ka.astype(BF16)
    base_sb = base + 2 * H_FOX * HEAD_DIM
    for j in range(H_SB):
        q_ref[H_FOX + j] = head(base_sb + HEAD_DIM * j).astype(BF16)
        k_ref[H_FOX + j] = head(base_sb + HEAD_DIM * (H_SB + j)).astype(BF16)


def _prep(fs, lf, b, s, tc, tk):
    nh = H_FOX + H_SB
    nk = s // tk
    return pl.pallas_call(
        functools.partial(_prep_kernel, tc=tc, tk=tk),
        grid=(b, s // tc),
        in_specs=[pl.BlockSpec((None, tc, W_FS), lambda bi, si: (bi, si, 0)),
                  pl.BlockSpec((None, tc, LANES), lambda bi, si: (bi, si, 0))],
        out_specs=[pl.BlockSpec((None, nh, tc, LANES), lambda bi, si: (bi, 0, si, 0)),
                   pl.BlockSpec((None, nh, tc, LANES), lambda bi, si: (bi, 0, si, 0)),
                   pl.BlockSpec((None, nh, tc // tk, HEAD_DIM, tk), lambda bi, si: (bi, 0, si, 0, 0))],
        out_shape=[jax.ShapeDtypeStruct((b, nh, s, LANES), BF16),
                   jax.ShapeDtypeStruct((b, nh, s, LANES), BF16),
                   jax.ShapeDtypeStruct((b, nh, nk, HEAD_DIM, tk), BF16)],
        scratch_shapes=[pltpu.VMEM((8, LANES), F32)],
        compiler_params=_cparams(("parallel", "arbitrary")),
        name="prep",
    )(fs.reshape(b, s, W_FS), lf.reshape(b, s, LANES))


def _attn_kernel(*refs, kind, nc, tq, tk):
    if kind == "sb":
        q_ref, k_ref, v_ref, g_ref, up_ref, o_ref = refs
        up = up_ref[...]
    else:
        q_ref, k_ref, v_ref, g_ref, o_ref = refs
    r = tq // tk
    nfull = pl.program_id(2) * (nc * r)
    row = lax.broadcasted_iota(jnp.int32, (tk, tq), 0)
    col = lax.broadcasted_iota(jnp.int32, (tk, tq), 1)
    qs = [q_ref[c * tq:(c + 1) * tq, :] for c in range(nc)]

    def fox_step(st, vt, carry, rel):
        m, l, acc = carry
        if rel is not None:
            st = jnp.where(row + rel <= col, st, NEG)
        m_new = jnp.maximum(m, jnp.max(st, axis=0, keepdims=True))
        alpha = jnp.exp(m - m_new)
        pt = jnp.exp(st - m_new)
        l = alpha * l + jnp.sum(pt, axis=0, keepdims=True)
        acc = alpha * acc + _dot(vt, pt.astype(BF16))
        return m_new, l, acc

    def sb_block(scores, vt, carries, rels):
        sps, laters, out = {}, {}, {}
        for c, z in scores.items():
            sp = jnp.maximum(z, 0.0) + jnp.log(1.0 + jnp.exp(-jnp.abs(z)))
            if rels[c] is not None:
                sp = jnp.where(row + rels[c] < col, sp, 0.0)
            sps[c] = sp
        for c, sp in sps.items():
            hi = sp.astype(BF16)
            lo = (sp - hi.astype(F32)).astype(BF16)
            laters[c] = _dot(up, hi) + _dot(up, lo)
        for c, z in scores.items():
            rs, acc = carries[c]
            a = jnp.exp(z - sps[c] - (laters[c] + rs))
            if rels[c] is not None:
                a = jnp.where(row + rels[c] < col, a, 0.0)
            out[c] = (rs + laters[c][0:1, :] + sps[c][0:1, :], acc + _dot(vt, a.astype(BF16)))
        return out

    def block(j, carries, jj):
        kb = k_ref[pl.ds(pl.multiple_of(j * tk, tk), tk), :]
        vt = v_ref[j]
        live = [c for c in range(nc) if jj is None or jj < (c + 1) * r]
        rels = {c: None if jj is None or jj < c * r else jj * tk - c * tq for c in live}
        scores = {c: _dot_nt(kb, qs[c]) for c in live}
        out = list(carries)
        if kind == "sb":
            new = sb_block(scores, vt, carries, rels)
        else:
            new = {c: fox_step(scores[c], vt, carries[c], rels[c]) for c in live}
        for c in live:
            out[c] = new[c]
        return tuple(out)

    zeros = jnp.zeros((HEAD_DIM, tq), F32)
    if kind == "sb":
        carries = tuple((jnp.zeros((1, tq), F32), zeros) for _ in range(nc))
        for jj in reversed(range(nc * r)):
            carries = block(nfull + jj, carries, jj)
        carries = lax.fori_loop(0, nfull, lambda i, cs: block(nfull - 1 - i, cs, None), carries)
        outs = [acc for _, acc in carries]
    else:
        carries = tuple((jnp.full((1, tq), NEG, F32), jnp.zeros((1, tq), F32), zeros) for _ in range(nc))
        carries = lax.fori_loop(0, nfull, lambda j, cs: block(j, cs, None), carries)
        for jj in range(nc * r):
            carries = block(nfull + jj, carries, jj)
        outs = [acc / l for _, l, acc in carries]

    g = g_ref[pl.program_id(1)]
    for c, o in enumerate(outs):
        ms = jnp.mean(o * o, axis=0, keepdims=True)
        o_ref[:, c * tq:(c + 1) * tq] = (o * lax.rsqrt(ms + EPS) * g).astype(BF16)


def _attn(q, k, vt, gains, kind, nc, tq, tk):
    b, _, s, _ = q.shape
    nk = s // tk
    tile = nc * tq
    base = H_FOX if kind == "sb" else 0
    in_specs = [pl.BlockSpec((None, None, tile, LANES), lambda bi, h, qi: (bi, base + h, qi, 0)),
                pl.BlockSpec((None, None, s, LANES), lambda bi, h, qi: (bi, base + h, 0, 0)),
                pl.BlockSpec((None, None, nk, HEAD_DIM, tk), lambda bi, h, qi: (bi, base + h, 0, 0, 0)),
                pl.BlockSpec(gains.shape, lambda bi, h, qi: (0, 0, 0))]
    args = [q, k, vt, gains]
    if kind == "sb":
        upper = jnp.triu(jnp.ones((tk, tk), F32), k=1).astype(BF16)
        in_specs.append(pl.BlockSpec((tk, tk), lambda bi, h, qi: (0, 0)))
        args.append(upper)
    return pl.pallas_call(
        functools.partial(_attn_kernel, kind=kind, nc=nc, tq=tq, tk=tk),
        grid=(b, H_FOX, s // tile),
        in_specs=in_specs,
        out_specs=pl.BlockSpec((None, None, HEAD_DIM, tile), lambda bi, h, qi: (bi, h, 0, qi)),
        out_shape=jax.ShapeDtypeStruct((b, H_FOX, HEAD_DIM, s), BF16),
        compiler_params=_cparams(("parallel", "parallel", "arbitrary")),
        name=kind + "_attn",
    )(*args)


def _dilated_kernel(*refs, tl, has_prev, is_last):
    it = iter(refs)
    q_ref, km_ref, kh_ref, vm_ref, vh_ref, bm_ref = (next(it) for _ in range(6))
    po_ref = pl_ref = g_ref = None
    if has_prev:
        po_ref, pl_ref = next(it), next(it)
    if is_last:
        g_ref = next(it)
    o_ref = next(it)
    lse_ref = None if is_last else next(it)

    first = pl.program_id(2) == 0
    lane = lax.broadcasted_iota(jnp.int32, (1, LANES), 1)
    half_mask = [(lane < HEAD_DIM).astype(F32), (lane >= HEAD_DIM).astype(F32)]
    lane_q = lax.broadcasted_iota(jnp.int32, (BAND, LANES), 1)
    colk = lax.broadcasted_iota(jnp.int32, (BAND, 2 * BAND), 1)

    kf = jnp.concatenate([kh_ref[...], km_ref[...]], axis=0).astype(F32)
    vf = jnp.concatenate([vh_ref[...], vm_ref[...]], axis=0).astype(F32)

    for jj in range(tl // BAND):
        rows = slice(jj * BAND, (jj + 1) * BAND)
        krows = slice(jj * BAND, jj * BAND + 2 * BAND)
        lse_tile = jnp.zeros((BAND, LANES), F32)
        for s in range(H_DIL // 2):
            lanes = slice(LANES * s, LANES * (s + 1))
            qb = q_ref[rows, lanes]
            o_slab = jnp.zeros((BAND, LANES), F32)
            for par in range(2):
                h = 2 * s + par
                kk = (kf[krows, lanes] * half_mask[par]).astype(BF16)
                vv = (vf[krows, lanes] * half_mask[par]).astype(BF16)
                sc = _dot_nt(qb, kk) + bm_ref[h]
                if jj == 0:
                    sc = jnp.where(jnp.logical_and(first, colk < BAND), NEG, sc)
                m = jnp.max(sc, axis=-1, keepdims=True)
                pe = jnp.exp(sc - m)
                den = jnp.sum(pe, axis=-1, keepdims=True)
                o = _dot(pe.astype(BF16), vv) / den
                lse = m + jnp.log(den)
                if has_prev:
                    lp = pl_ref[rows, :][:, h:h + 1]
                    mm = jnp.maximum(lp, lse)
                    e1 = jnp.exp(lp - mm)
                    e2 = jnp.exp(lse - mm)
                    o = (e1 * (po_ref[rows, lanes] * half_mask[par]) + e2 * o) / (e1 + e2)
                    lse = mm + jnp.log(e1 + e2)
                if is_last:
                    ms = jnp.sum(o * o, axis=-1, keepdims=True) * (1.0 / HEAD_DIM)
                    o = o * lax.rsqrt(ms + EPS)
                else:
                    lse_tile = jnp.where(lane_q == h, lse, lse_tile)
                o_slab = o_slab + o
            if is_last:
                o_ref[rows, lanes] = (o_slab * g_ref[:, lanes]).astype(o_ref.dtype)
            else:
                o_ref[rows, lanes] = o_slab
        if not is_last:
            lse_ref[rows, :] = lse_tile


def _dilated(dil, prev, bm, gains, b, s, d, is_last):
    l = s // d
    tl = min(512, l)
    nsub = tl // BAND
    has_prev = prev is not None
    view = dil.reshape(b, l, d * 3 * W_DIL)
    main = lambda part: pl.BlockSpec((None, tl, W_DIL), lambda bi, r, i: (bi, i, 3 * r + part))
    halo = lambda part: pl.BlockSpec((None, BAND, W_DIL),
                                     lambda bi, r, i: (bi, jnp.maximum(i * nsub - 1, 0), 3 * r + part))
    ospec = pl.BlockSpec((None, tl, W_DIL), lambda bi, r, i: (bi, i, r))
    lspec = pl.BlockSpec((None, tl, LANES), lambda bi, r, i: (bi, i, r))
    in_specs = [main(0), main(1), halo(1), main(2), halo(2), pl.BlockSpec(bm.shape, lambda bi, r, i: (0, 0, 0))]
    args = [view, view, view, view, view, bm]
    if has_prev:
        in_specs += [ospec, lspec]
        args += [prev[0].reshape(b, l, d * W_DIL), prev[1].reshape(b, l, d * LANES)]
    if is_last:
        in_specs.append(pl.BlockSpec(gains.shape, lambda bi, r, i: (0, 0)))
        args.append(gains)
        out_specs = ospec
        out_shape = jax.ShapeDtypeStruct((b, l, d * W_DIL), BF16)
    else:
        out_specs = [ospec, lspec]
        out_shape = [jax.ShapeDtypeStruct((b, l, d * W_DIL), F32), jax.ShapeDtypeStruct((b, l, d * LANES), F32)]
    out = pl.pallas_call(
        functools.partial(_dilated_kernel, tl=tl, has_prev=has_prev, is_last=is_last),
        grid=(b, d, l // tl),
        in_specs=in_specs,
        out_specs=out_specs,
        out_shape=out_shape,
        compiler_params=_cparams(("parallel", "parallel", "arbitrary")),
        name=f"dilated_d{d}",
    )(*args)
    if is_last:
        return out.reshape(b, s, W_DIL)
    return out[0].reshape(b * s, W_DIL), out[1].reshape(b * s, LANES)


def _alibi_bias(d):
    slopes = 2.0 ** (-8.0 * jnp.arange(1, H_DIL + 1, dtype=F32) / H_DIL)
    dist = jnp.arange(BAND)[:, None] - jnp.arange(2 * BAND)[None, :] + BAND
    valid = (dist >= 0) & (dist <= BAND)
    bias = -slopes[:, None, None] * (dist * d).astype(F32)[None]
    return jnp.where(valid[None], bias, NEG)


def _outproj_kernel(h_ref, od_ref, of_ref, os_ref, wd_ref, wf_ref, ws_ref, o_ref):
    o_ref[...] = (h_ref[...] + _dot(od_ref[...], wd_ref[...]) + _dot_tn(of_ref[...], wf_ref[...])
                  + _dot_tn(os_ref[...], ws_ref[...]))


def _outproj(h, od, of, os_, wd, wf, ws, b, s, tm):
    full = lambda a: pl.BlockSpec(a.shape, lambda bi, i: (0, 0))
    tile = lambda w: pl.BlockSpec((None, tm, w), lambda bi, i: (bi, i, 0))
    tile_t = pl.BlockSpec((None, W_FOX, tm), lambda bi, i: (bi, 0, i))
    return pl.pallas_call(
        _outproj_kernel,
        grid=(b, s // tm),
        in_specs=[tile(D_MODEL), tile(W_DIL), tile_t, tile_t, full(wd), full(wf), full(ws)],
        out_specs=tile(D_MODEL),
        out_shape=jax.ShapeDtypeStruct((b, s, D_MODEL), F32),
        compiler_params=_cparams(("parallel", "parallel")),
        name="outproj",
    )(h.reshape(b, s, D_MODEL), od, of, os_, wd, wf, ws).reshape(b * s, D_MODEL)


def _swiglu_tile(u, wg_ref, wu_ref, wd_ref, hid_ref):
    for c in range(0, D_FF, 256):
        g = _dot(u, wg_ref[:, c:c + 256])
        v = _dot(u, wu_ref[:, c:c + 256])
        hid_ref[:, c:c + 256] = (g * (1.0 / (1.0 + jnp.exp(-g))) * v).astype(BF16)
    return _dot(hid_ref[...], wd_ref[...])


def _ffn_kernel(x_ref, g_ref, wg_ref, wu_ref, wd_ref, o_ref, u_ref, hid_ref):
    x = x_ref[...]
    u_ref[...] = _rms(x, g_ref[...]).astype(BF16)
    o_ref[...] = x + _swiglu_tile(u_ref[...], wg_ref, wu_ref, wd_ref, hid_ref)


def _ffn(h, g, wg, wu, wd, tm):
    t = h.shape[0]
    full = lambda a: pl.BlockSpec(a.shape, lambda i: (0, 0))
    tile = pl.BlockSpec((tm, D_MODEL), lambda i: (i, 0))
    return pl.pallas_call(
        _ffn_kernel,
        grid=(t // tm,),
        in_specs=[tile, full(g), full(wg), full(wu), full(wd)],
        out_specs=tile,
        out_shape=jax.ShapeDtypeStruct((t, D_MODEL), F32),
        scratch_shapes=[pltpu.VMEM((tm, D_MODEL), BF16), pltpu.VMEM((tm, D_FF), BF16)],
        compiler_params=_cparams(("parallel",)),
        name="ffn",
    )(h, g, wg, wu, wd)


def _moe_kernel(x_ref, g_ref, rw_ref, wg_ref, wu_ref, wd_ref, fg_ref, o_ref, u_ref, hid_ref, comb_ref, acc_ref,
                *, final_norm):
    e = pl.program_id(1)
    lane = lax.broadcasted_iota(jnp.int32, comb_ref.shape, 1)

    @pl.when(e == 0)
    def _():
        u = _rms(x_ref[...], g_ref[...])
        u_ref[...] = u.astype(BF16)
        logits = jnp.dot(u, rw_ref[...], preferred_element_type=F32, precision=lax.Precision.HIGHEST)
        logits = jnp.where(lane < N_EXPERTS, logits, NEG)
        t1 = jnp.max(logits, axis=-1, keepdims=True)
        i1 = jnp.min(jnp.where(logits == t1, lane, LANES), axis=-1, keepdims=True)
        rest = jnp.where(lane == i1, NEG, logits)
        t2 = jnp.max(rest, axis=-1, keepdims=True)
        i2 = jnp.min(jnp.where(rest == t2, lane, LANES), axis=-1, keepdims=True)
        e2 = jnp.exp(t2 - t1)
        comb_ref[...] = jnp.where(lane == i1, 1.0 / (1.0 + e2), 0.0) + jnp.where(lane == i2, e2 / (1.0 + e2), 0.0)
        acc_ref[...] = jnp.zeros_like(acc_ref)

    y = _swiglu_tile(u_ref[...], wg_ref, wu_ref, wd_ref, hid_ref)
    w = jnp.sum(jnp.where(lane == e, comb_ref[...], 0.0), axis=-1, keepdims=True)
    acc_ref[...] += w * y

    @pl.when(e == N_EXPERTS - 1)
    def _():
        res = x_ref[...] + acc_ref[...]
        if final_norm:
            res = _rms(res, fg_ref[...])
        o_ref[...] = res


def _moe(h, g, rw, wg, wu, wd, fg, tm, final_norm):
    t = h.shape[0]
    full = lambda a: pl.BlockSpec(a.shape, lambda i, e: (0, 0))
    tile = pl.BlockSpec((tm, D_MODEL), lambda i, e: (i, 0))
    expert = lambda a: pl.BlockSpec((None,) + a.shape[1:], lambda i, e: (e, 0, 0))
    return pl.pallas_call(
        functools.partial(_moe_kernel, final_norm=final_norm),
        grid=(t // tm, N_EXPERTS),
        in_specs=[tile, full(g), full(rw), expert(wg), expert(wu), expert(wd), full(fg)],
        out_specs=tile,
        out_shape=jax.ShapeDtypeStruct((t, D_MODEL), F32),
        scratch_shapes=[pltpu.VMEM((tm, D_MODEL), BF16), pltpu.VMEM((tm, D_FF), BF16),
                        pltpu.VMEM((tm, LANES), F32), pltpu.VMEM((tm, D_MODEL), F32)],
        compiler_params=_cparams(("parallel", "arbitrary")),
        name="moe",
    )(h, g, rw, wg, wu, wd, fg)


def _rmsnorm_kernel(x_ref, g_ref, o_ref):
    o_ref[...] = _rms(x_ref[...], g_ref[...])


def _rmsnorm(h, g, tm):
    t = h.shape[0]
    tile = pl.BlockSpec((tm, D_MODEL), lambda i: (i, 0))
    return pl.pallas_call(
        _rmsnorm_kernel, grid=(t // tm,),
        in_specs=[tile, pl.BlockSpec(g.shape, lambda i: (0, 0))], out_specs=tile,
        out_shape=jax.ShapeDtypeStruct((t, D_MODEL), F32),
        compiler_params=_cparams(("parallel",)), name="final_norm",
    )(h, g)


def _mixer_weights(w_in, forget_bias, head_norm, w_out, tq):
    d = D_MODEL
    a, b = W_DIL, W_DIL + W_FOX
    wq, wk, wv, wl = w_in[:, :d] * 0.125, w_in[:, d:2 * d], w_in[:, 2 * d:3 * d], w_in[:, 3 * d:]
    wd = jnp.concatenate([wq[:, :a], wk[:, :a], wv[:, :a]], axis=1).astype(BF16)
    wf = jnp.concatenate([wv[:, a:b], wv[:, b:], wq[:, a:b], wk[:, a:b], wq[:, b:], wk[:, b:]], axis=1).astype(BF16)
    wl = jnp.pad(wl, ((0, 0), (0, LANES - H_FOX))).astype(BF16)
    fb = jnp.pad(forget_bias, (0, LANES - H_FOX)).reshape(1, LANES)
    gcol = lambda g: jnp.broadcast_to(g.reshape(H_FOX, HEAD_DIM, 1), (H_FOX, HEAD_DIM, tq))
    return dict(wd=wd, wf=wf, wl=wl, fb=fb, wo_d=w_out[:a].astype(BF16), wo_f=w_out[a:b].astype(BF16),
                wo_s=w_out[b:].astype(BF16),
                g_dil=head_norm[:a].reshape(1, a), g_fox=gcol(head_norm[a:b]), g_sb=gcol(head_norm[b:]))


def _token_mixer(h, norm_g, w, bias_mats, b, s, tm, nc, tq, tk):
    dil, fs, lf = _inproj(h, norm_g, w["wd"], w["wf"], w["wl"], w["fb"], tm)
    q, k, vt = _prep(fs, lf, b, s, min(512, s), tk)
    prev = None
    for pi, (_, d) in enumerate(DIL_PATTERNS):
        is_last = pi == len(DIL_PATTERNS) - 1
        prev = _dilated(dil, prev, bias_mats[pi], w["g_dil"], b, s, d, is_last)
    o_dil = prev
    o_fox = _attn(q, k, vt, w["g_fox"], "fox", nc, tq, tk).reshape(b, W_FOX, s)
    o_sb = _attn(q, k, vt, w["g_sb"], "sb", nc, tq, tk).reshape(b, W_FOX, s)
    return _outproj(h, o_dil, o_fox, o_sb, w["wo_d"], w["wo_f"], w["wo_s"], b, s, tm)


def kernel(x, mix_norm, w_in, forget_bias, head_norm, w_out, ffn_norm, dense_w_gate, dense_w_up, dense_w_down,
           router_w, moe_w_gate, moe_w_up, moe_w_down, final_norm):
    b, s, d = x.shape
    depth = mix_norm.shape[0]
    tm, nc, tq, tk = 512, 4, 512, 256
    h = x.reshape(b * s, d)
    bias_mats = [_alibi_bias(dd) for _, dd in DIL_PATTERNS]
    fg = final_norm.reshape(1, d)
    for layer in range(depth):
        w = _mixer_weights(w_in[layer], forget_bias[layer], head_norm[layer], w_out[layer], tq)
        h = _token_mixer(h, mix_norm[layer].reshape(1, d), w, bias_mats, b, s, tm, nc, tq, tk)
        g = ffn_norm[layer].reshape(1, d)
        i = layer // 2
        if layer % 2 == 0:
            h = _ffn(h, g, dense_w_gate[i].astype(BF16), dense_w_up[i].astype(BF16), dense_w_down[i].astype(BF16), tm)
        else:
            rw = jnp.pad(router_w[i], ((0, 0), (0, LANES - N_EXPERTS)))
            h = _moe(h, g, rw, moe_w_gate[i].astype(BF16), moe_w_up[i].astype(BF16), moe_w_down[i].astype(BF16),
                     fg, tm, final_norm=(layer == depth - 1))
    if depth % 2 == 1:
        h = _rmsnorm(h, fg, tm)
    return h.reshape(b, s, d)
```

```python
import functools

import jax
import jax.numpy as jnp
from jax import lax
from jax.experimental import pallas as pl
from jax.experimental.pallas import tpu as pltpu

D_MODEL = 1024
HEAD_DIM = 64
H_DIL, H_FOX, H_SB = 6, 5, 5
DIL_PATTERNS = ((128, 1), (512, 4), (2048, 16))
BAND = 128
D_FF = 2816
N_EXPERTS = 8
EPS = 1e-6
NEG = -1e30
LANES = 128
W_DIL = H_DIL * HEAD_DIM
W_FOX = H_FOX * HEAD_DIM
W_FS = 3 * (H_FOX + H_SB) * HEAD_DIM
VMEM_LIMIT = 56 * 1024 * 1024

F32 = jnp.float32
BF16 = jnp.bfloat16
NT = (((1,), (1,)), ((), ()))
TN = (((0,), (0,)), ((), ()))


def _dot(a, b):
    return jnp.dot(a, b, preferred_element_type=F32)


def _dot_nt(a, b):
    return lax.dot_general(a, b, NT, preferred_element_type=F32)


def _dot_tn(a, b):
    return lax.dot_general(a, b, TN, preferred_element_type=F32)


def _rms(x, g):
    return x * lax.rsqrt(jnp.mean(x * x, axis=-1, keepdims=True) + EPS) * g


def _split3(x):
    x1 = x.astype(BF16)
    r = x - x1.astype(F32)
    x2 = r.astype(BF16)
    x3 = (r - x2.astype(F32)).astype(BF16)
    return x1, x2, x3


def _cparams(sem):
    return pltpu.CompilerParams(dimension_semantics=sem, vmem_limit_bytes=VMEM_LIMIT)


def _inproj_kernel(x_ref, g_ref, wd_ref, wf_ref, wl_ref, fb_ref, dil_ref, fs_ref, lf_ref, u_ref):
    u_ref[...] = _rms(x_ref[...], g_ref[...]).astype(BF16)
    u = u_ref[...]
    for w_ref, o_ref in ((wd_ref, dil_ref), (wf_ref, fs_ref)):
        n = w_ref.shape[1]
        for c in range(0, n, 512):
            e = min(c + 512, n)
            o_ref[:, c:e] = _dot(u, w_ref[:, c:e]).astype(BF16)
    z = _dot(u, wl_ref[...]) + fb_ref[...]
    lf_ref[...] = jnp.minimum(z, 0.0) - jnp.log1p(jnp.exp(-jnp.abs(z)))


def _inproj(h, g, wd, wf, wl, fb, tm):
    t = h.shape[0]
    full = lambda a: pl.BlockSpec(a.shape, lambda i: (0, 0))
    return pl.pallas_call(
        _inproj_kernel,
        grid=(t // tm,),
        in_specs=[pl.BlockSpec((tm, D_MODEL), lambda i: (i, 0)), full(g), full(wd), full(wf), full(wl), full(fb)],
        out_specs=[pl.BlockSpec((tm, 3 * W_DIL), lambda i: (i, 0)),
                   pl.BlockSpec((tm, W_FS), lambda i: (i, 0)),
                   pl.BlockSpec((tm, LANES), lambda i: (i, 0))],
        out_shape=[jax.ShapeDtypeStruct((t, 3 * W_DIL), BF16),
                   jax.ShapeDtypeStruct((t, W_FS), BF16),
                   jax.ShapeDtypeStruct((t, LANES), F32)],
        scratch_shapes=[pltpu.VMEM((tm, D_MODEL), BF16)],
        compiler_params=_cparams(("parallel",)),
        name="inproj",
    )(h, g, wd, wf, wl, fb)


def _prep_kernel(fs_ref, lf_ref, q_ref, k_ref, vt_ref, carry_ref, *, tc, tk):
    @pl.when(pl.program_id(1) == 0)
    def _():
        carry_ref[...] = jnp.zeros_like(carry_ref)

    row = lax.broadcasted_iota(jnp.int32, (tc, tc), 0)
    col = lax.broadcasted_iota(jnp.int32, (tc, tc), 1)
    tri = jnp.where(col <= row, 1.0, 0.0).astype(BF16)
    x1, x2, x3 = _split3(lf_ref[...])
    c = _dot(tri, x1) + _dot(tri, x2) + _dot(tri, x3) + carry_ref[0:1, :]
    carry_ref[0:1, :] = c[tc - 1:tc, :]

    n_heads = H_FOX + H_SB
    for s in range(n_heads // 2):
        vt = fs_ref[:, LANES * s:LANES * (s + 1)].astype(F32).T
        for half in range(2):
            for ci in range(tc // tk):
                vt_ref[2 * s + half, ci] = vt[HEAD_DIM * half:HEAD_DIM * (half + 1), ci * tk:(ci + 1) * tk].astype(BF16)

    lane = lax.broadcasted_iota(jnp.int32, (tc, LANES), 1)
    low = lane < HEAD_DIM

    def head(col0):
        s = col0 // LANES
        x = fs_ref[:, LANES * s:LANES * (s + 1)].astype(F32)
        if (col0 // HEAD_DIM) % 2 == 1:
            x = pltpu.roll(x, HEAD_DIM, 1)
        return jnp.where(low, x, 0.0)

    base = n_heads * HEAD_DIM
    for j in range(H_FOX):
        cj = c[:, j:j + 1]
        c1 = cj.astype(BF16).astype(F32)
        r = cj - c1
        c2 = r.astype(BF16).astype(F32)
        c3 = r - c2
        qa = head(base + HEAD_DIM * j)
        ka = head(base + HEAD_DIM * (H_FOX + j))
        for off, (qv, kv) in enumerate(((c1, 1.0), (c2, 1.0), (c3, 1.0), (1.0, -c1), (1.0, -c2), (1.0, -c3))):
            sel = lane == HEAD_DIM + off
            qa = jnp.where(sel, qv, qa)
            ka = jnp.where(sel, kv, ka)
        q_ref[j] = qa.astype(BF16)
        k_ref[j] = ka.astype(BF16)
    base_sb = base + 2 * H_FOX * HEAD_DIM
    for j in range(H_SB):
        q_ref[H_FOX + j] = head(base_sb + HEAD_DIM * j).astype(BF16)
        k_ref[H_FOX + j] = head(base_sb + HEAD_DIM * (H_SB + j)).astype(BF16)


def _prep(fs, lf, b, s, tc, tk):
    nh = H_FOX + H_SB
    nk = s // tk
    return pl.pallas_call(
        functools.partial(_prep_kernel, tc=tc, tk=tk),
        grid=(b, s // tc),
        in_specs=[pl.BlockSpec((None, tc, W_FS), lambda bi, si: (bi, si, 0)),
                  pl.BlockSpec((None, tc, LANES), lambda bi, si: (bi, si, 0))],
        out_specs=[pl.BlockSpec((None, nh, tc, LANES), lambda bi, si: (bi, 0, si, 0)),
                   pl.BlockSpec((None, nh, tc, LANES), lambda bi, si: (bi, 0, si, 0)),
                   pl.BlockSpec((None, nh, tc // tk, HEAD_DIM, tk), lambda bi, si: (bi, 0, si, 0, 0))],
        out_shape=[jax.ShapeDtypeStruct((b, nh, s, LANES), BF16),
                   jax.ShapeDtypeStruct((b, nh, s, LANES), BF16),
                   jax.ShapeDtypeStruct((b, nh, nk, HEAD_DIM, tk), BF16)],
        scratch_shapes=[pltpu.VMEM((8, LANES), F32)],
        compiler_params=_cparams(("parallel", "arbitrary")),
        name="prep",
    )(fs.reshape(b, s, W_FS), lf.reshape(b, s, LANES))


def _attn_kernel(*refs, kind, nc, tq, tk):
    if kind == "sb":
        q_ref, k_ref, v_ref, g_ref, up_ref, o_ref = refs
        up = up_ref[...]
    else:
        q_ref, k_ref, v_ref, g_ref, o_ref = refs
    r = tq // tk
    nfull = pl.program_id(2) * (nc * r)
    row = lax.broadcasted_iota(jnp.int32, (tk, tq), 0)
    col = lax.broadcasted_iota(jnp.int32, (tk, tq), 1)
    qs = [q_ref[c * tq:(c + 1) * tq, :] for c in range(nc)]

    def fox_step(st, vt, carry, rel):
        m, l, acc = carry
        if rel is not None:
            st = jnp.where(row + rel <= col, st, NEG)
        m_new = jnp.maximum(m, jnp.max(st, axis=0, keepdims=True))
        alpha = jnp.exp(m - m_new)
        pt = jnp.exp(st - m_new)
        l = alpha * l + jnp.sum(pt, axis=0, keepdims=True)
        acc = alpha * acc + _dot(vt, pt.astype(BF16))
        return m_new, l, acc

    def sb_block(scores, vt, carries, rels):
        sps, laters, out = {}, {}, {}
        for c, z in scores.items():
            sp = jnp.maximum(z, 0.0) + jnp.log(1.0 + jnp.exp(-jnp.abs(z)))
            if rels[c] is not None:
                sp = jnp.where(row + rels[c] < col, sp, 0.0)
            sps[c] = sp
        for c, sp in sps.items():
            hi = sp.astype(BF16)
            lo = (sp - hi.astype(F32)).astype(BF16)
            laters[c] = _dot(up, hi) + _dot(up, lo)
        for c, z in scores.items():
            rs, acc = carries[c]
            a = jnp.exp(z - sps[c] - (laters[c] + rs))
            if rels[c] is not None:
                a = jnp.where(row + rels[c] < col, a, 0.0)
            out[c] = (rs + laters[c][0:1, :] + sps[c][0:1, :], acc + _dot(vt, a.astype(BF16)))
        return out

    def block(j, carries, jj):
        kb = k_ref[pl.ds(pl.multiple_of(j * tk, tk), tk), :]
        vt = v_ref[j]
        live = [c for c in range(nc) if jj is None or jj < (c + 1) * r]
        rels = {c: None if jj is None or jj < c * r else jj * tk - c * tq for c in live}
        scores = {c: _dot_nt(kb, qs[c]) for c in live}
        out = list(carries)
        if kind == "sb":
            new = sb_block(scores, vt, carries, rels)
        else:
            new = {c: fox_step(scores[c], vt, carries[c], rels[c]) for c in live}
        for c in live:
            out[c] = new[c]
        return tuple(out)

    zeros = jnp.zeros((HEAD_DIM, tq), F32)
    if kind == "sb":
        carries = tuple((jnp.zeros((1, tq), F32), zeros) for _ in range(nc))
        for jj in reversed(range(nc * r)):
            carries = block(nfull + jj, carries, jj)
        carries = lax.fori_loop(0, nfull, lambda i, cs: block(nfull - 1 - i, cs, None), carries)
        outs = [acc for _, acc in carries]
    else:
        carries = tuple((jnp.full((1, tq), NEG, F32), jnp.zeros((1, tq), F32), zeros) for _ in range(nc))
        carries = lax.fori_loop(0, nfull, lambda j, cs: block(j, cs, None), carries)
        for jj in range(nc * r):
            carries = block(nfull + jj, carries, jj)
        outs = [acc / l for _, l, acc in carries]

    g = g_ref[pl.program_id(1)]
    for c, o in enumerate(outs):
        ms = jnp.mean(o * o, axis=0, keepdims=True)
        o_ref[:, c * tq:(c + 1) * tq] = (o * lax.rsqrt(ms + EPS) * g).astype(BF16)


def _attn(q, k, vt, gains, kind, nc, tq, tk):
    b, _, s, _ = q.shape
    nk = s // tk
    tile = nc * tq
    base = H_FOX if kind == "sb" else 0
    in_specs = [pl.BlockSpec((None, None, tile, LANES), lambda bi, h, qi: (bi, base + h, qi, 0)),
                pl.BlockSpec((None, None, s, LANES), lambda bi, h, qi: (bi, base + h, 0, 0)),
                pl.BlockSpec((None, None, nk, HEAD_DIM, tk), lambda bi, h, qi: (bi, base + h, 0, 0, 0)),
                pl.BlockSpec(gains.shape, lambda bi, h, qi: (0, 0, 0))]
    args = [q, k, vt, gains]
    if kind == "sb":
        upper = jnp.triu(jnp.ones((tk, tk), F32), k=1).astype(BF16)
        in_specs.append(pl.BlockSpec((tk, tk), lambda bi, h, qi: (0, 0)))
        args.append(upper)
    return pl.pallas_call(
        functools.partial(_attn_kernel, kind=kind, nc=nc, tq=tq, tk=tk),
        grid=(b, H_FOX, s // tile),
        in_specs=in_specs,
        out_specs=pl.BlockSpec((None, None, HEAD_DIM, tile), lambda bi, h, qi: (bi, h, 0, qi)),
        out_shape=jax.ShapeDtypeStruct((b, H_FOX, HEAD_DIM, s), BF16),
        compiler_params=_cparams(("parallel", "parallel", "arbitrary")),
        name=kind + "_attn",
    )(*args)


def _dilated_kernel(*refs, tl, has_prev, is_last):
    it = iter(refs)
    q_ref, km_ref, kh_ref, vm_ref, vh_ref, bm_ref = (next(it) for _ in range(6))
    po_ref = pl_ref = g_ref = None
    if has_prev:
        po_ref, pl_ref = next(it), next(it)
    if is_last:
        g_ref = next(it)
    o_ref = next(it)
    lse_ref = None if is_last else next(it)

    first = pl.program_id(2) == 0
    lane = lax.broadcasted_iota(jnp.int32, (1, LANES), 1)
    half_mask = [(lane < HEAD_DIM).astype(F32), (lane >= HEAD_DIM).astype(F32)]
    lane_q = lax.broadcasted_iota(jnp.int32, (BAND, LANES), 1)
    colk = lax.broadcasted_iota(jnp.int32, (BAND, 2 * BAND), 1)

    kf = jnp.concatenate([kh_ref[...], km_ref[...]], axis=0).astype(F32)
    vf = jnp.concatenate([vh_ref[...], vm_ref[...]], axis=0).astype(F32)

    for jj in range(tl // BAND):
        rows = slice(jj * BAND, (jj + 1) * BAND)
        krows = slice(jj * BAND, jj * BAND + 2 * BAND)
        lse_tile = jnp.zeros((BAND, LANES), F32)
        for s in range(H_DIL // 2):
            lanes = slice(LANES * s, LANES * (s + 1))
            qb = q_ref[rows, lanes]
            o_slab = jnp.zeros((BAND, LANES), F32)
            for par in range(2):
                h = 2 * s + par
                kk = (kf[krows, lanes] * half_mask[par]).astype(BF16)
                vv = (vf[krows, lanes] * half_mask[par]).astype(BF16)
                sc = _dot_nt(qb, kk) + bm_ref[h]
                if jj == 0:
                    sc = jnp.where(jnp.logical_and(first, colk < BAND), NEG, sc)
                m = jnp.max(sc, axis=-1, keepdims=True)
                pe = jnp.exp(sc - m)
                den = jnp.sum(pe, axis=-1, keepdims=True)
                o = _dot(pe.astype(BF16), vv) / den
                lse = m + jnp.log(den)
                if has_prev:
                    lp = pl_ref[rows, :][:, h:h + 1]
                    mm = jnp.maximum(lp, lse)
                    e1 = jnp.exp(lp - mm)
                    e2 = jnp.exp(lse - mm)
                    o = (e1 * (po_ref[rows, lanes] * half_mask[par]) + e2 * o) / (e1 + e2)
                    lse = mm + jnp.log(e1 + e2)
                if is_last:
                    ms = jnp.sum(o * o, axis=-1, keepdims=True) * (1.0 / HEAD_DIM)
                    o = o * lax.rsqrt(ms + EPS)
                else:
                    lse_tile = jnp.where(lane_q == h, lse, lse_tile)
                o_slab = o_slab + o
            if is_last:
                o_ref[rows, lanes] = (o_slab * g_ref[:, lanes]).astype(o_ref.dtype)
            else:
                o_ref[rows, lanes] = o_slab
        if not is_last:
            lse_ref[rows, :] = lse_tile


def _dilated(dil, prev, bm, gains, b, s, d, is_last):
    l = s // d
    tl = min(512, l)
    nsub = tl // BAND
    has_prev = prev is not None
    view = dil.reshape(b, l, d * 3 * W_DIL)
    main = lambda part: pl.BlockSpec((None, tl, W_DIL), lambda bi, r, i: (bi, i, 3 * r + part))
    halo = lambda part: pl.BlockSpec((None, BAND, W_DIL),
                                     lambda bi, r, i: (bi, jnp.maximum(i * nsub - 1, 0), 3 * r + part))
    ospec = pl.BlockSpec((None, tl, W_DIL), lambda bi, r, i: (bi, i, r))
    lspec = pl.BlockSpec((None, tl, LANES), lambda bi, r, i: (bi, i, r))
    in_specs = [main(0), main(1), halo(1), main(2), halo(2), pl.BlockSpec(bm.shape, lambda bi, r, i: (0, 0, 0))]
    args = [view, view, view, view, view, bm]
    if has_prev:
        in_specs += [ospec, lspec]
        args += [prev[0].reshape(b, l, d * W_DIL), prev[1].reshape(b, l, d * LANES)]
    if is_last:
        in_specs.append(pl.BlockSpec(gains.shape, lambda bi, r, i: (0, 0)))
        args.append(gains)
        out_specs = ospec
        out_shape = jax.ShapeDtypeStruct((b, l, d * W_DIL), BF16)
    else:
        out_specs = [ospec, lspec]
        out_shape = [jax.ShapeDtypeStruct((b, l, d * W_DIL), F32), jax.ShapeDtypeStruct((b, l, d * LANES), F32)]
    out = pl.pallas_call(
        functools.partial(_dilated_kernel, tl=tl, has_prev=has_prev, is_last=is_last),
        grid=(b, d, l // tl),
        in_specs=in_specs,
        out_specs=out_specs,
        out_shape=out_shape,
        compiler_params=_cparams(("parallel", "parallel", "arbitrary")),
        name=f"dilated_d{d}",
    )(*args)
    if is_last:
        return out.reshape(b, s, W_DIL)
    return out[0].reshape(b * s, W_DIL), out[1].reshape(b * s, LANES)


def _alibi_bias(d):
    slopes = 2.0 ** (-8.0 * jnp.arange(1, H_DIL + 1, dtype=F32) / H_DIL)
    dist = jnp.arange(BAND)[:, None] - jnp.arange(2 * BAND)[None, :] + BAND
    valid = (dist >= 0) & (dist <= BAND)
    bias = -slopes[:, None, None] * (dist * d).astype(F32)[None]
    return jnp.where(valid[None], bias, NEG)


def _dilated_all_kernel(q_ref, kc_ref, kp_ref, vc_ref, vp_ref, bm_ref, g_ref, o_ref, qf, kf, vf, of, lf, *, tp):
    first = pl.program_id(1) == 0
    n_slab = H_DIL // 2
    for s in range(n_slab):
        lanes = slice(LANES * s, LANES * (s + 1))
        qf[s] = q_ref[:, lanes].astype(F32)
        kf[s, 0:tp] = kp_ref[:, lanes].astype(F32)
        kf[s, tp:2 * tp] = kc_ref[:, lanes].astype(F32)
        vf[s, 0:tp] = vp_ref[:, lanes].astype(F32)
        vf[s, tp:2 * tp] = vc_ref[:, lanes].astype(F32)

    lane = lax.broadcasted_iota(jnp.int32, (1, LANES), 1)
    half_mask = [(lane < HEAD_DIM).astype(F32), (lane >= HEAD_DIM).astype(F32)]
    lane_q = lax.broadcasted_iota(jnp.int32, (BAND, LANES), 1)
    colk = lax.broadcasted_iota(jnp.int32, (BAND, 2 * BAND), 1)
    heads = [(s, par) for s in range(n_slab) for par in range(2)]

    for pi, (_, d) in enumerate(DIL_PATTERNS):
        has_prev = pi > 0
        is_last = pi == len(DIL_PATTERNS) - 1

        def block(it, carry, d=d, pi=pi, has_prev=has_prev, is_last=is_last):
            q0 = it % d + (it // d) * (d * BAND)
            k0 = tp + q0 - d * BAND
            no_halo = jnp.logical_and(first, it < d)
            rows_q = pl.ds(q0, BAND, stride=d)
            rows_k = pl.ds(k0, 2 * BAND, stride=d)
            scores, vals = {}, {}
            for s, par in heads:
                qb = qf[s, rows_q, :].astype(BF16)
                kk = (kf[s, rows_k, :] * half_mask[par]).astype(BF16)
                sc = _dot_nt(qb, kk) + bm_ref[pi, 2 * s + par]
                scores[s, par] = jnp.where(jnp.logical_and(no_halo, colk < BAND), NEG, sc)
                vals[s, par] = (vf[s, rows_k, :] * half_mask[par]).astype(BF16)
            lse_tile = jnp.zeros((BAND, LANES), F32)
            o_slab = [jnp.zeros((BAND, LANES), F32) for _ in range(n_slab)]
            for s, par in heads:
                h = 2 * s + par
                sc = scores[s, par]
                m = jnp.max(sc, axis=-1, keepdims=True)
                pe = jnp.exp(sc - m)
                den = jnp.sum(pe, axis=-1, keepdims=True)
                o = _dot(pe.astype(BF16), vals[s, par]) / den
                lse = m + jnp.log(den)
                if has_prev:
                    lp = lf[rows_q, :][:, h:h + 1]
                    mm = jnp.maximum(lp, lse)
                    e1 = jnp.exp(lp - mm)
                    e2 = jnp.exp(lse - mm)
                    o = (e1 * (of[s, rows_q, :] * half_mask[par]) + e2 * o) / (e1 + e2)
                    lse = mm + jnp.log(e1 + e2)
                if is_last:
                    ms = jnp.sum(o * o, axis=-1, keepdims=True) * (1.0 / HEAD_DIM)
                    o = o * lax.rsqrt(ms + EPS)
                else:
                    lse_tile = jnp.where(lane_q == h, lse, lse_tile)
                o_slab[s] = o_slab[s] + o
            for s in range(n_slab):
                of[s, rows_q, :] = o_slab[s]
            if not is_last:
                lf[rows_q, :] = lse_tile
            return carry

        lax.fori_loop(0, tp // BAND, block, 0)

    for s in range(n_slab):
        lanes = slice(LANES * s, LANES * (s + 1))
        o_ref[:, lanes] = (of[s] * g_ref[:, lanes]).astype(BF16)


def _dilated_all(dil, bm, gains, b, s, tp):
    view = dil.reshape(b, s, 3 * W_DIL)
    cur = lambda part: pl.BlockSpec((None, tp, W_DIL), lambda bi, i: (bi, i, part))
    prev = lambda part: pl.BlockSpec((None, tp, W_DIL), lambda bi, i: (bi, jnp.maximum(i - 1, 0), part))
    n_slab = H_DIL // 2
    return pl.pallas_call(
        functools.partial(_dilated_all_kernel, tp=tp),
        grid=(b, s // tp),
        in_specs=[cur(0), cur(1), prev(1), cur(2), prev(2),
                  pl.BlockSpec(bm.shape, lambda bi, i: (0, 0, 0, 0)), pl.BlockSpec(gains.shape, lambda bi, i: (0, 0))],
        out_specs=pl.BlockSpec((None, tp, W_DIL), lambda bi, i: (bi, i, 0)),
        out_shape=jax.ShapeDtypeStruct((b, s, W_DIL), BF16),
        scratch_shapes=[pltpu.VMEM((n_slab, tp, LANES), F32), pltpu.VMEM((n_slab, 2 * tp, LANES), F32),
                        pltpu.VMEM((n_slab, 2 * tp, LANES), F32), pltpu.VMEM((n_slab, tp, LANES), F32),
                        pltpu.VMEM((tp, LANES), F32)],
        compiler_params=_cparams(("parallel", "arbitrary")),
        name="dilated",
    )(view, view, view, view, view, bm, gains)


def _outproj_kernel(h_ref, od_ref, of_ref, os_ref, wd_ref, wf_ref, ws_ref, o_ref):
    o_ref[...] = (h_ref[...] + _dot(od_ref[...], wd_ref[...]) + _dot_tn(of_ref[...], wf_ref[...])
                  + _dot_tn(os_ref[...], ws_ref[...]))


def _outproj(h, od, of, os_, wd, wf, ws, b, s, tm):
    full = lambda a: pl.BlockSpec(a.shape, lambda bi, i: (0, 0))
    tile = lambda w: pl.BlockSpec((None, tm, w), lambda bi, i: (bi, i, 0))
    tile_t = pl.BlockSpec((None, W_FOX, tm), lambda bi, i: (bi, 0, i))
    return pl.pallas_call(
        _outproj_kernel,
        grid=(b, s // tm),
        in_specs=[tile(D_MODEL), tile(W_DIL), tile_t, tile_t, full(wd), full(wf), full(ws)],
        out_specs=tile(D_MODEL),
        out_shape=jax.ShapeDtypeStruct((b, s, D_MODEL), F32),
        compiler_params=_cparams(("parallel", "parallel")),
        name="outproj",
    )(h.reshape(b, s, D_MODEL), od, of, os_, wd, wf, ws).reshape(b * s, D_MODEL)


def _swiglu_tile(u, wg_ref, wu_ref, wd_ref, hid_ref):
    for c in range(0, D_FF, 256):
        g = _dot(u, wg_ref[:, c:c + 256])
        v = _dot(u, wu_ref[:, c:c + 256])
        hid_ref[:, c:c + 256] = (g * (1.0 / (1.0 + jnp.exp(-g))) * v).astype(BF16)
    return _dot(hid_ref[...], wd_ref[...])


def _ffn_kernel(x_ref, g_ref, wg_ref, wu_ref, wd_ref, o_ref, u_ref, hid_ref):
    x = x_ref[...]
    u_ref[...] = _rms(x, g_ref[...]).astype(BF16)
    o_ref[...] = x + _swiglu_tile(u_ref[...], wg_ref, wu_ref, wd_ref, hid_ref)


def _ffn(h, g, wg, wu, wd, tm):
    t = h.shape[0]
    full = lambda a: pl.BlockSpec(a.shape, lambda i: (0, 0))
    tile = pl.BlockSpec((tm, D_MODEL), lambda i: (i, 0))
    return pl.pallas_call(
        _ffn_kernel,
        grid=(t // tm,),
        in_specs=[tile, full(g), full(wg), full(wu), full(wd)],
        out_specs=tile,
        out_shape=jax.ShapeDtypeStruct((t, D_MODEL), F32),
        scratch_shapes=[pltpu.VMEM((tm, D_MODEL), BF16), pltpu.VMEM((tm, D_FF), BF16)],
        compiler_params=_cparams(("parallel",)),
        name="ffn",
    )(h, g, wg, wu, wd)


R_I1, R_I2, R_G1, R_G2, R_K1, R_K2 = 8, 9, 10, 11, 12, 13


def _router_kernel(x_ref, g_ref, rw_ref, route_ref, cnt_ref, carry_ref, *, tm):
    @pl.when(pl.program_id(0) == 0)
    def _():
        carry_ref[...] = jnp.zeros_like(carry_ref)

    lane = lax.broadcasted_iota(jnp.int32, (tm, LANES), 1)
    u = _rms(x_ref[...], g_ref[...])
    logits = jnp.dot(u, rw_ref[...], preferred_element_type=F32, precision=lax.Precision.HIGHEST)
    logits = jnp.where(lane < N_EXPERTS, logits, NEG)
    t1 = jnp.max(logits, axis=-1, keepdims=True)
    i1 = jnp.min(jnp.where(logits == t1, lane, LANES), axis=-1, keepdims=True)
    rest = jnp.where(lane == i1, NEG, logits)
    t2 = jnp.max(rest, axis=-1, keepdims=True)
    i2 = jnp.min(jnp.where(rest == t2, lane, LANES), axis=-1, keepdims=True)
    e2 = jnp.exp(t2 - t1)
    g1 = 1.0 / (1.0 + e2)
    g2 = e2 / (1.0 + e2)

    onehot = jnp.where(lane == i1, 1.0, 0.0) + jnp.where(lane == i2, 1.0, 0.0)
    row = lax.broadcasted_iota(jnp.int32, (tm, tm), 0)
    col = lax.broadcasted_iota(jnp.int32, (tm, tm), 1)
    strict = jnp.where(col < row, 1.0, 0.0).astype(BF16)
    before = _dot(strict, onehot.astype(BF16)) + carry_ref[0:1, :]
    k1 = jnp.sum(jnp.where(lane == i1, before, 0.0), axis=-1, keepdims=True)
    k2 = jnp.sum(jnp.where(lane == i2, before, 0.0), axis=-1, keepdims=True)
    carry_ref[0:1, :] = before[tm - 1:tm, :] + onehot[tm - 1:tm, :]

    rec = jnp.zeros((tm, LANES), F32)
    for ln, val in ((R_I1, i1.astype(F32)), (R_I2, i2.astype(F32)), (R_G1, g1), (R_G2, g2), (R_K1, k1), (R_K2, k2)):
        rec = jnp.where(lane == ln, val, rec)
    route_ref[...] = rec
    cnt_ref[...] = jnp.broadcast_to(carry_ref[0:1, :], cnt_ref.shape)


def _router(h, g, rw, tm):
    t = h.shape[0]
    full = lambda a: pl.BlockSpec(a.shape, lambda i: (0, 0))
    return pl.pallas_call(
        functools.partial(_router_kernel, tm=tm),
        grid=(t // tm,),
        in_specs=[pl.BlockSpec((tm, D_MODEL), lambda i: (i, 0)), full(g), full(rw)],
        out_specs=[pl.BlockSpec((tm, LANES), lambda i: (i, 0)), pl.BlockSpec((8, LANES), lambda i: (0, 0))],
        out_shape=[jax.ShapeDtypeStruct((t, LANES), F32), jax.ShapeDtypeStruct((8, LANES), F32)],
        scratch_shapes=[pltpu.VMEM((8, LANES), F32)],
        compiler_params=_cparams(("arbitrary",)),
        name="router",
    )(h, g, rw)


def _dispatch_kernel(pos_ref, x_ref, g_ref, xs_in_ref, xs_ref, ubuf, sem, *, tm):
    del xs_in_ref
    i = pl.program_id(0)
    slot = i % 2
    ubuf[slot] = _rms(x_ref[...], g_ref[...])

    def issue(r, carry):
        for k in range(2):
            dst = pos_ref[0, k * tm + r]
            pltpu.make_async_copy(ubuf.at[slot, pl.ds(r, 1), :], xs_ref.at[pl.ds(dst, 1), :], sem.at[slot]).start()
        return carry

    lax.fori_loop(0, tm, issue, 0)

    def wait_slot(s):
        for _ in range(2):
            pltpu.make_async_copy(ubuf.at[s], ubuf.at[s], sem.at[s]).wait()

    @pl.when(i > 0)
    def _():
        wait_slot(1 - slot)

    @pl.when(i == pl.num_programs(0) - 1)
    def _():
        wait_slot(slot)


def _dispatch(h, g, pos_tiles, xs_zero, tm):
    t = h.shape[0]
    return pl.pallas_call(
        functools.partial(_dispatch_kernel, tm=tm),
        grid=(t // tm,),
        in_specs=[pl.BlockSpec((None, 1, 2 * tm), lambda i: (i, 0, 0), memory_space=pltpu.SMEM),
                  pl.BlockSpec((tm, D_MODEL), lambda i: (i, 0)),
                  pl.BlockSpec(g.shape, lambda i: (0, 0)),
                  pl.BlockSpec(memory_space=pl.ANY)],
        out_specs=pl.BlockSpec(memory_space=pl.ANY),
        out_shape=jax.ShapeDtypeStruct(xs_zero.shape, F32),
        scratch_shapes=[pltpu.VMEM((2, tm, D_MODEL), F32), pltpu.SemaphoreType.DMA((2,))],
        input_output_aliases={3: 0},
        compiler_params=_cparams(("arbitrary",)),
        name="dispatch",
    )(pos_tiles, h, g, xs_zero)


def _experts_kernel(te_ref, nreal_ref, x_ref, wg_ref, wu_ref, wd_ref, y_ref, hid_ref):
    del te_ref
    used = pl.program_id(0) < nreal_ref[0]

    @pl.when(used)
    def _():
        y_ref[...] = _swiglu_tile(x_ref[...].astype(BF16), wg_ref, wu_ref, wd_ref, hid_ref)

    @pl.when(jnp.logical_not(used))
    def _():
        y_ref[...] = jnp.zeros_like(y_ref)


def _experts(xs, tile_expert, n_real, wg, wu, wd, tm):
    p = xs.shape[0]
    rows = lambda i, te, nr: (jnp.minimum(i, nr[0] - 1), 0)
    expert = lambda a: pl.BlockSpec((None,) + a.shape[1:], lambda i, te, nr: (te[i], 0, 0))
    return pl.pallas_call(
        _experts_kernel,
        grid_spec=pltpu.PrefetchScalarGridSpec(
            num_scalar_prefetch=2,
            grid=(p // tm,),
            in_specs=[pl.BlockSpec((tm, D_MODEL), rows), expert(wg), expert(wu), expert(wd)],
            out_specs=pl.BlockSpec((tm, D_MODEL), lambda i, te, nr: (i, 0)),
            scratch_shapes=[pltpu.VMEM((tm, D_FF), BF16)]),
        out_shape=jax.ShapeDtypeStruct((p, D_MODEL), F32),
        compiler_params=_cparams(("arbitrary",)),
        name="experts",
    )(tile_expert, n_real, xs, wg, wu, wd)


def _combine_kernel(pos_ref, posn_ref, h_ref, route_ref, fg_ref, ys_ref, o_ref, ybuf, sem, *, tm, final_norm):
    i = pl.program_id(0)
    slot = i % 2

    def gather(p_ref, s):
        def issue(r, carry):
            pltpu.make_async_copy(ys_ref.at[pl.ds(p_ref[0, r], 1), :], ybuf.at[s, pl.ds(r, 1), :], sem.at[s]).start()
            return carry
        lax.fori_loop(0, 2 * tm, issue, 0)

    @pl.when(i == 0)
    def _():
        gather(pos_ref, slot)

    @pl.when(i + 1 < pl.num_programs(0))
    def _():
        gather(posn_ref, 1 - slot)

    pltpu.make_async_copy(ybuf.at[slot], ybuf.at[slot], sem.at[slot]).wait()
    route = route_ref[...]
    y = route[:, R_G1:R_G1 + 1] * ybuf[slot, 0:tm, :] + route[:, R_G2:R_G2 + 1] * ybuf[slot, tm:2 * tm, :]
    res = h_ref[...] + y
    if final_norm:
        res = _rms(res, fg_ref[...])
    o_ref[...] = res


def _combine(h, route, fg, pos_tiles, ys, tm, final_norm):
    t = h.shape[0]
    n = t // tm
    tile = lambda w: pl.BlockSpec((tm, w), lambda i: (i, 0))
    smem = lambda f: pl.BlockSpec((None, 1, 2 * tm), lambda i: (f(i), 0, 0), memory_space=pltpu.SMEM)
    return pl.pallas_call(
        functools.partial(_combine_kernel, tm=tm, final_norm=final_norm),
        grid=(n,),
        in_specs=[smem(lambda i: i), smem(lambda i: jnp.minimum(i + 1, n - 1)), tile(D_MODEL), tile(LANES),
                  pl.BlockSpec(fg.shape, lambda i: (0, 0)), pl.BlockSpec(memory_space=pl.ANY)],
        out_specs=tile(D_MODEL),
        out_shape=jax.ShapeDtypeStruct((t, D_MODEL), F32),
        scratch_shapes=[pltpu.VMEM((2, 2 * tm, D_MODEL), F32), pltpu.SemaphoreType.DMA((2,))],
        compiler_params=_cparams(("arbitrary",)),
        name="combine",
    )(pos_tiles, pos_tiles, h, route, fg, ys)


def _moe(h, g, rw, wg, wu, wd, fg, tm, final_norm):
    t = h.shape[0]
    n_tiles = 2 * t // tm + N_EXPERTS
    route, cnt = _router(h, g, rw, tm)
    counts = cnt[0, :N_EXPERTS].astype(jnp.int32)
    padded = (counts + tm - 1) // tm * tm
    ends = jnp.cumsum(padded)
    starts = ends - padded
    chosen = route[:, R_I1:R_I2 + 1].astype(jnp.int32)
    pos = jnp.sum(jnp.where(chosen[..., None] == jnp.arange(N_EXPERTS), starts, 0), axis=-1)
    pos = pos + route[:, R_K1:R_K2 + 1].astype(jnp.int32)
    pos_tiles = pos.reshape(t // tm, tm, 2).transpose(0, 2, 1).reshape(t // tm, 1, 2 * tm)
    n_real = ends[-1:] // tm
    tile_ids = jnp.minimum(jnp.arange(n_tiles), n_real[0] - 1)
    tile_expert = jnp.sum(tile_ids[:, None] * tm >= ends[None, :], axis=-1).astype(jnp.int32)

    xs = _dispatch(h, g, pos_tiles, jnp.zeros((n_tiles * tm, D_MODEL), F32), tm)
    ys = _experts(xs, tile_expert, n_real.astype(jnp.int32), wg, wu, wd, tm)
    return _combine(h, route, fg, pos_tiles, ys, tm, final_norm)


def _rmsnorm_kernel(x_ref, g_ref, o_ref):
    o_ref[...] = _rms(x_ref[...], g_ref[...])


def _rmsnorm(h, g, tm):
    t = h.shape[0]
    tile = pl.BlockSpec((tm, D_MODEL), lambda i: (i, 0))
    return pl.pallas_call(
        _rmsnorm_kernel, grid=(t // tm,),
        in_specs=[tile, pl.BlockSpec(g.shape, lambda i: (0, 0))], out_specs=tile,
        out_shape=jax.ShapeDtypeStruct((t, D_MODEL), F32),
        compiler_params=_cparams(("parallel",)), name="final_norm",
    )(h, g)


def _mixer_weights(w_in, forget_bias, head_norm, w_out, tq):
    d = D_MODEL
    a, b = W_DIL, W_DIL + W_FOX
    wq, wk, wv, wl = w_in[:, :d] * 0.125, w_in[:, d:2 * d], w_in[:, 2 * d:3 * d], w_in[:, 3 * d:]
    wd = jnp.concatenate([wq[:, :a], wk[:, :a], wv[:, :a]], axis=1).astype(BF16)
    wf = jnp.concatenate([wv[:, a:b], wv[:, b:], wq[:, a:b], wk[:, a:b], wq[:, b:], wk[:, b:]], axis=1).astype(BF16)
    wl = jnp.pad(wl, ((0, 0), (0, LANES - H_FOX))).astype(BF16)
    fb = jnp.pad(forget_bias, (0, LANES - H_FOX)).reshape(1, LANES)
    gcol = lambda g: jnp.broadcast_to(g.reshape(H_FOX, HEAD_DIM, 1), (H_FOX, HEAD_DIM, tq))
    return dict(wd=wd, wf=wf, wl=wl, fb=fb, wo_d=w_out[:a].astype(BF16), wo_f=w_out[a:b].astype(BF16),
                wo_s=w_out[b:].astype(BF16),
                g_dil=head_norm[:a].reshape(1, a), g_fox=gcol(head_norm[a:b]), g_sb=gcol(head_norm[b:]))


def _token_mixer(h, norm_g, w, bias_mats, b, s, tm, nc, tq, tk):
    dil, fs, lf = _inproj(h, norm_g, w["wd"], w["wf"], w["wl"], w["fb"], tm)
    q, k, vt = _prep(fs, lf, b, s, min(512, s), tk)
    o_dil = _dilated_all(dil, bias_mats, w["g_dil"], b, s, BAND * DIL_PATTERNS[-1][1])
    o_fox = _attn(q, k, vt, w["g_fox"], "fox", nc, tq, tk).reshape(b, W_FOX, s)
    o_sb = _attn(q, k, vt, w["g_sb"], "sb", nc, tq, tk).reshape(b, W_FOX, s)
    return _outproj(h, o_dil, o_fox, o_sb, w["wo_d"], w["wo_f"], w["wo_s"], b, s, tm)


def kernel(x, mix_norm, w_in, forget_bias, head_norm, w_out, ffn_norm, dense_w_gate, dense_w_up, dense_w_down,
           router_w, moe_w_gate, moe_w_up, moe_w_down, final_norm):
    b, s, d = x.shape
    depth = mix_norm.shape[0]
    tm, nc, tq, tk = 512, 4, 512, 256
    h = x.reshape(b * s, d)
    bias_mats = jnp.stack([_alibi_bias(dd) for _, dd in DIL_PATTERNS])
    fg = final_norm.reshape(1, d)
    for layer in range(depth):
        w = _mixer_weights(w_in[layer], forget_bias[layer], head_norm[layer], w_out[layer], tq)
        h = _token_mixer(h, mix_norm[layer].reshape(1, d), w, bias_mats, b, s, tm, nc, tq, tk)
        g = ffn_norm[layer].reshape(1, d)
        i = layer // 2
        if layer % 2 == 0:
            h = _ffn(h, g, dense_w_gate[i].astype(BF16), dense_w_up[i].astype(BF16), dense_w_down[i].astype(BF16), tm)
        else:
            rw = jnp.pad(router_w[i], ((0, 0), (0, LANES - N_EXPERTS)))
            h = _moe(h, g, rw, moe_w_gate[i].astype(BF16), moe_w_up[i].astype(BF16), moe_w_down[i].astype(BF16),
                     fg, tm, final_norm=(layer == depth - 1))
    if depth % 2 == 1:
        h = _rmsnorm(h, fg, tm)
    return h.reshape(b, s, d)
```

```python
import functools

import jax
import jax.numpy as jnp
from jax import lax
from jax.experimental import pallas as pl
from jax.experimental.pallas import tpu as pltpu

D_MODEL = 1024
HEAD_DIM = 64
H_DIL, H_FOX, H_SB = 6, 5, 5
DIL_PATTERNS = ((128, 1), (512, 4), (2048, 16))
BAND = 128
D_FF = 2816
N_EXPERTS = 8
EPS = 1e-6
NEG = -1e30
LANES = 128
W_DIL = H_DIL * HEAD_DIM
W_FOX = H_FOX * HEAD_DIM
W_FS = 3 * (H_FOX + H_SB) * HEAD_DIM
VMEM_LIMIT = 56 * 1024 * 1024

F32 = jnp.float32
BF16 = jnp.bfloat16
NT = (((1,), (1,)), ((), ()))
TN = (((0,), (0,)), ((), ()))


def _dot(a, b):
    return jnp.dot(a, b, preferred_element_type=F32)


def _dot_nt(a, b):
    return lax.dot_general(a, b, NT, preferred_element_type=F32)


def _dot_tn(a, b):
    return lax.dot_general(a, b, TN, preferred_element_type=F32)


def _rms(x, g):
    return x * lax.rsqrt(jnp.mean(x * x, axis=-1, keepdims=True) + EPS) * g


def _split3(x):
    x1 = x.astype(BF16)
    r = x - x1.astype(F32)
    x2 = r.astype(BF16)
    x3 = (r - x2.astype(F32)).astype(BF16)
    return x1, x2, x3


def _cparams(sem):
    return pltpu.CompilerParams(dimension_semantics=sem, vmem_limit_bytes=VMEM_LIMIT)


def _inproj_kernel(x_ref, g_ref, wd_ref, wf_ref, wl_ref, fb_ref, dil_ref, fs_ref, lf_ref, u_ref):
    u_ref[...] = _rms(x_ref[...], g_ref[...]).astype(BF16)
    u = u_ref[...]
    for w_ref, o_ref in ((wd_ref, dil_ref), (wf_ref, fs_ref)):
        n = w_ref.shape[1]
        for c in range(0, n, 512):
            e = min(c + 512, n)
            o_ref[:, c:e] = _dot(u, w_ref[:, c:e]).astype(BF16)
    z = _dot(u, wl_ref[...]) + fb_ref[...]
    lf_ref[...] = jnp.minimum(z, 0.0) - jnp.log1p(jnp.exp(-jnp.abs(z)))


def _inproj(h, g, wd, wf, wl, fb, tm):
    t = h.shape[0]
    full = lambda a: pl.BlockSpec(a.shape, lambda i: (0, 0))
    return pl.pallas_call(
        _inproj_kernel,
        grid=(t // tm,),
        in_specs=[pl.BlockSpec((tm, D_MODEL), lambda i: (i, 0)), full(g), full(wd), full(wf), full(wl), full(fb)],
        out_specs=[pl.BlockSpec((tm, 3 * W_DIL), lambda i: (i, 0)),
                   pl.BlockSpec((tm, W_FS), lambda i: (i, 0)),
                   pl.BlockSpec((tm, LANES), lambda i: (i, 0))],
        out_shape=[jax.ShapeDtypeStruct((t, 3 * W_DIL), BF16),
                   jax.ShapeDtypeStruct((t, W_FS), BF16),
                   jax.ShapeDtypeStruct((t, LANES), F32)],
        scratch_shapes=[pltpu.VMEM((tm, D_MODEL), BF16)],
        compiler_params=_cparams(("parallel",)),
        name="inproj",
    )(h, g, wd, wf, wl, fb)


def _prep_kernel(fs_ref, lf_ref, q_ref, k_ref, vt_ref, carry_ref, *, tc, tk):
    @pl.when(pl.program_id(1) == 0)
    def _():
        carry_ref[...] = jnp.zeros_like(carry_ref)

    row = lax.broadcasted_iota(jnp.int32, (tc, tc), 0)
    col = lax.broadcasted_iota(jnp.int32, (tc, tc), 1)
    tri = jnp.where(col <= row, 1.0, 0.0).astype(BF16)
    x1, x2, x3 = _split3(lf_ref[...])
    c = _dot(tri, x1) + _dot(tri, x2) + _dot(tri, x3) + carry_ref[0:1, :]
    carry_ref[0:1, :] = c[tc - 1:tc, :]

    n_heads = H_FOX + H_SB
    for s in range(n_heads // 2):
        vt = fs_ref[:, LANES * s:LANES * (s + 1)].astype(F32).T
        for half in range(2):
            for ci in range(tc // tk):
                vt_ref[2 * s + half, ci] = vt[HEAD_DIM * half:HEAD_DIM * (half + 1), ci * tk:(ci + 1) * tk].astype(BF16)

    lane = lax.broadcasted_iota(jnp.int32, (tc, LANES), 1)
    low = lane < HEAD_DIM

    def head(col0):
        s = col0 // LANES
        x = fs_ref[:, LANES * s:LANES * (s + 1)].astype(F32)
        if (col0 // HEAD_DIM) % 2 == 1:
            x = pltpu.roll(x, HEAD_DIM, 1)
        return jnp.where(low, x, 0.0)

    base = n_heads * HEAD_DIM
    for j in range(H_FOX):
        cj = c[:, j:j + 1]
        c1 = cj.astype(BF16).astype(F32)
        r = cj - c1
        c2 = r.astype(BF16).astype(F32)
        c3 = r - c2
        qa = head(base + HEAD_DIM * j)
        ka = head(base + HEAD_DIM * (H_FOX + j))
        for off, (qv, kv) in enumerate(((c1, 1.0), (c2, 1.0), (c3, 1.0), (1.0, -c1), (1.0, -c2), (1.0, -c3))):
            sel = lane == HEAD_DIM + off
            qa = jnp.where(sel, qv, qa)
            ka = jnp.where(sel, kv, ka)
        q_ref[j] = qa.astype(BF16)
        k_ref[j] = ka.astype(BF16)
    base_sb = base + 2 * H_FOX * HEAD_DIM
    for j in range(H_SB):
        q_ref[H_FOX + j] = head(base_sb + HEAD_DIM * j).astype(BF16)
        k_ref[H_FOX + j] = head(base_sb + HEAD_DIM * (H_SB + j)).astype(BF16)


def _prep(fs, lf, b, s, tc, tk):
    nh = H_FOX + H_SB
    nk = s // tk
    return pl.pallas_call(
        functools.partial(_prep_kernel, tc=tc, tk=tk),
        grid=(b, s // tc),
        in_specs=[pl.BlockSpec((None, tc, W_FS), lambda bi, si: (bi, si, 0)),
                  pl.BlockSpec((None, tc, LANES), lambda bi, si: (bi, si, 0))],
        out_specs=[pl.BlockSpec((None, nh, tc, LANES), lambda bi, si: (bi, 0, si, 0)),
                   pl.BlockSpec((None, nh, tc, LANES), lambda bi, si: (bi, 0, si, 0)),
                   pl.BlockSpec((None, nh, tc // tk, HEAD_DIM, tk), lambda bi, si: (bi, 0, si, 0, 0))],
        out_shape=[jax.ShapeDtypeStruct((b, nh, s, LANES), BF16),
                   jax.ShapeDtypeStruct((b, nh, s, LANES), BF16),
                   jax.ShapeDtypeStruct((b, nh, nk, HEAD_DIM, tk), BF16)],
        scratch_shapes=[pltpu.VMEM((8, LANES), F32)],
        compiler_params=_cparams(("parallel", "arbitrary")),
        name="prep",
    )(fs.reshape(b, s, W_FS), lf.reshape(b, s, LANES))


def _attn_kernel(*refs, kind, nc, tq, tk):
    if kind == "sb":
        q_ref, k_ref, v_ref, g_ref, up_ref, o_ref = refs
        up = up_ref[...]
    else:
        q_ref, k_ref, v_ref, g_ref, o_ref = refs
    r = tq // tk
    nfull = pl.program_id(2) * (nc * r)
    row = lax.broadcasted_iota(jnp.int32, (tk, tq), 0)
    col = lax.broadcasted_iota(jnp.int32, (tk, tq), 1)
    qs = [q_ref[c * tq:(c + 1) * tq, :] for c in range(nc)]

    def fox_step(st, vt, carry, rel):
        m, l, acc = carry
        if rel is not None:
            st = jnp.where(row + rel <= col, st, NEG)
        m_new = jnp.maximum(m, jnp.max(st, axis=0, keepdims=True))
        alpha = jnp.exp(m - m_new)
        pt = jnp.exp(st - m_new)
        l = alpha * l + jnp.sum(pt, axis=0, keepdims=True)
        acc = alpha * acc + _dot(vt, pt.astype(BF16))
        return m_new, l, acc

    def sb_block(scores, vt, carries, rels):
        sps, laters, out = {}, {}, {}
        for c, z in scores.items():
            sp = jnp.maximum(z, 0.0) + jnp.log(1.0 + jnp.exp(-jnp.abs(z)))
            if rels[c] is not None:
                sp = jnp.where(row + rels[c] < col, sp, 0.0)
            sps[c] = sp
        for c, sp in sps.items():
            hi = sp.astype(BF16)
            lo = (sp - hi.astype(F32)).astype(BF16)
            laters[c] = _dot(up, hi) + _dot(up, lo)
        for c, z in scores.items():
            rs, acc = carries[c]
            a = jnp.exp(z - sps[c] - (laters[c] + rs))
            if rels[c] is not None:
                a = jnp.where(row + rels[c] < col, a, 0.0)
            out[c] = (rs + laters[c][0:1, :] + sps[c][0:1, :], acc + _dot(vt, a.astype(BF16)))
        return out

    def block(j, carries, jj):
        kb = k_ref[pl.ds(pl.multiple_of(j * tk, tk), tk), :]
        vt = v_ref[j]
        live = [c for c in range(nc) if jj is None or jj < (c + 1) * r]
        rels = {c: None if jj is None or jj < c * r else jj * tk - c * tq for c in live}
        scores = {c: _dot_nt(kb, qs[c]) for c in live}
        out = list(carries)
        if kind == "sb":
            new = sb_block(scores, vt, carries, rels)
        else:
            new = {c: fox_step(scores[c], vt, carries[c], rels[c]) for c in live}
        for c in live:
            out[c] = new[c]
        return tuple(out)

    zeros = jnp.zeros((HEAD_DIM, tq), F32)
    if kind == "sb":
        carries = tuple((jnp.zeros((1, tq), F32), zeros) for _ in range(nc))
        for jj in reversed(range(nc * r)):
            carries = block(nfull + jj, carries, jj)
        carries = lax.fori_loop(0, nfull, lambda i, cs: block(nfull - 1 - i, cs, None), carries)
        outs = [acc for _, acc in carries]
    else:
        carries = tuple((jnp.full((1, tq), NEG, F32), jnp.zeros((1, tq), F32), zeros) for _ in range(nc))
        carries = lax.fori_loop(0, nfull, lambda j, cs: block(j, cs, None), carries)
        for jj in range(nc * r):
            carries = block(nfull + jj, carries, jj)
        outs = [acc / l for _, l, acc in carries]

    g = g_ref[pl.program_id(1)]
    for c, o in enumerate(outs):
        ms = jnp.mean(o * o, axis=0, keepdims=True)
        o_ref[:, c * tq:(c + 1) * tq] = (o * lax.rsqrt(ms + EPS) * g).astype(BF16)


def _attn(q, k, vt, gains, kind, nc, tq, tk):
    b, _, s, _ = q.shape
    nk = s // tk
    tile = nc * tq
    base = H_FOX if kind == "sb" else 0
    in_specs = [pl.BlockSpec((None, None, tile, LANES), lambda bi, h, qi: (bi, base + h, qi, 0)),
                pl.BlockSpec((None, None, s, LANES), lambda bi, h, qi: (bi, base + h, 0, 0)),
                pl.BlockSpec((None, None, nk, HEAD_DIM, tk), lambda bi, h, qi: (bi, base + h, 0, 0, 0)),
                pl.BlockSpec(gains.shape, lambda bi, h, qi: (0, 0, 0))]
    args = [q, k, vt, gains]
    if kind == "sb":
        upper = jnp.triu(jnp.ones((tk, tk), F32), k=1).astype(BF16)
        in_specs.append(pl.BlockSpec((tk, tk), lambda bi, h, qi: (0, 0)))
        args.append(upper)
    return pl.pallas_call(
        functools.partial(_attn_kernel, kind=kind, nc=nc, tq=tq, tk=tk),
        grid=(b, H_FOX, s // tile),
        in_specs=in_specs,
        out_specs=pl.BlockSpec((None, None, HEAD_DIM, tile), lambda bi, h, qi: (bi, h, 0, qi)),
        out_shape=jax.ShapeDtypeStruct((b, H_FOX, HEAD_DIM, s), BF16),
        compiler_params=_cparams(("parallel", "parallel", "arbitrary")),
        name=kind + "_attn",
    )(*args)


LOG2E = 1.4426950408889634


def _neg_abs(x):
    return pltpu.bitcast(pltpu.bitcast(x, jnp.uint32) | jnp.uint32(0x80000000), F32)


def _attn2_kernel(*refs, kind, nc, tq, tk):
    if kind == "sb":
        q_ref, k_ref, v_ref, g_ref, up_ref, o_ref, sbuf = refs
        up2 = up_ref[...]
    else:
        q_ref, k_ref, v_ref, g_ref, o_ref, sbuf = refs
    r = tq // tk
    nt = nc * r
    nfull = pl.program_id(2) * nt
    row = lax.broadcasted_iota(jnp.int32, (tk, tq), 0)
    col = lax.broadcasted_iota(jnp.int32, (tk, tq), 1)
    qs = [q_ref[c * tq:(c + 1) * tq, :] for c in range(nc)]
    every = list(range(nc))
    live_at = lambda jj: [c for c in every if jj < (c + 1) * r]

    def scores_of(j, live):
        kb = k_ref[pl.ds(pl.multiple_of(j * tk, tk), tk), :]
        return {c: _dot_nt(kb, qs[c]) for c in live}

    def park(scores, slot):
        for c, sc in scores.items():
            sbuf[slot, c] = sc

    def parked(slot):
        return {c: sbuf[slot, c] for c in every}

    def fox_update(scores, vt, carries, rels):
        out = {}
        for c, st in scores.items():
            m, l, acc = carries[c]
            if rels[c] is not None:
                st = jnp.where(row + rels[c] <= col, st, NEG)
            m_new = jnp.maximum(m, jnp.max(st, axis=0, keepdims=True))
            alpha = jnp.exp(m - m_new)
            pt = jnp.exp(st - m_new)
            out[c] = (m_new, alpha * l + jnp.sum(pt, axis=0, keepdims=True), alpha * acc + _dot(vt, pt.astype(BF16)))
        return out

    def sb_update(scores, vt, carries, rels):
        sps, laters, out = {}, {}, {}
        for c, z in scores.items():
            e = jnp.exp2(_neg_abs(z * LOG2E))
            sp = jnp.maximum(z, 0.0) + jnp.log(1.0 + e)
            if rels[c] is not None:
                sp = jnp.where(row + rels[c] < col, sp, 0.0)
            sps[c] = sp
        for c, sp in sps.items():
            hi = sp.astype(BF16)
            lo = (sp - hi.astype(F32)).astype(BF16)
            laters[c] = _dot(up2, jnp.concatenate([hi, lo], axis=0))
        for c, z in scores.items():
            rs, acc = carries[c]
            a = jnp.exp(z - sps[c] - laters[c])
            if rels[c] is not None:
                a = jnp.where(row + rels[c] < col, a, 0.0)
            out[c] = (rs + laters[c][0:1, :] + sps[c][0:1, :], acc + _dot(vt, a.astype(BF16)) * jnp.exp(-rs))
        return out

    def update(scores, j, carries, jj):
        rels = {c: None if jj is None or jj < c * r else jj * tk - c * tq for c in scores}
        new = (sb_update if kind == "sb" else fox_update)(scores, v_ref[j], carries, rels)
        return tuple(new.get(c, carries[c]) for c in every)

    zeros = jnp.zeros((HEAD_DIM, tq), F32)
    if kind == "sb":
        carries = tuple((jnp.zeros((1, tq), F32), zeros) for _ in every)
        cur = scores_of(nfull + nt - 1, live_at(nt - 1))
        for jj in reversed(range(nt)):
            if jj > 0:
                nxt = scores_of(nfull + jj - 1, live_at(jj - 1))
            else:
                park(scores_of(jnp.maximum(nfull - 1, 0), every), 0)
            carries = update(cur, nfull + jj, carries, jj)
            cur = nxt

        def pair(i2, cs):
            for half in range(2):
                j = nfull - 1 - (2 * i2 + half)
                park(scores_of(jnp.maximum(j - 1, 0), every), 1 - half)
                cs = update(parked(half), j, cs, None)
            return cs

        carries = lax.fori_loop(0, nfull // 2, pair, carries)
        outs = [acc for _, acc in carries]
    else:
        carries = tuple((jnp.full((1, tq), NEG, F32), jnp.zeros((1, tq), F32), zeros) for _ in every)
        park(scores_of(0, every), 0)

        def pair(i2, cs):
            for half in range(2):
                j = 2 * i2 + half
                park(scores_of(j + 1, every), 1 - half)
                cs = update(parked(half), j, cs, None)
            return cs

        carries = lax.fori_loop(0, nfull // 2, pair, carries)
        cur = parked(0)
        for jj in range(nt):
            if jj + 1 < nt:
                nxt = scores_of(nfull + jj + 1, live_at(jj + 1))
            carries = update(cur, nfull + jj, carries, jj)
            cur = nxt
        outs = [acc / l for _, l, acc in carries]

    g = g_ref[pl.program_id(1)]
    for c, o in enumerate(outs):
        ms = jnp.mean(o * o, axis=0, keepdims=True)
        o_ref[:, c * tq:(c + 1) * tq] = (o * lax.rsqrt(ms + EPS) * g).astype(BF16)


def _attn2(q, k, vt, gains, kind, nc, tq, tk):
    b, _, s, _ = q.shape
    nk = s // tk
    tile = nc * tq
    assert (tile // tk) % 2 == 0
    base = H_FOX if kind == "sb" else 0
    in_specs = [pl.BlockSpec((None, None, tile, LANES), lambda bi, h, qi: (bi, base + h, qi, 0)),
                pl.BlockSpec((None, None, s, LANES), lambda bi, h, qi: (bi, base + h, 0, 0)),
                pl.BlockSpec((None, None, nk, HEAD_DIM, tk), lambda bi, h, qi: (bi, base + h, 0, 0, 0)),
                pl.BlockSpec(gains.shape, lambda bi, h, qi: (0, 0, 0))]
    args = [q, k, vt, gains]
    if kind == "sb":
        upper = jnp.triu(jnp.ones((tk, tk), F32), k=1).astype(BF16)
        in_specs.append(pl.BlockSpec((tk, 2 * tk), lambda bi, h, qi: (0, 0)))
        args.append(jnp.concatenate([upper, upper], axis=1))
    return pl.pallas_call(
        functools.partial(_attn2_kernel, kind=kind, nc=nc, tq=tq, tk=tk),
        grid=(b, H_FOX, s // tile),
        in_specs=in_specs,
        out_specs=pl.BlockSpec((None, None, HEAD_DIM, tile), lambda bi, h, qi: (bi, h, 0, qi)),
        out_shape=jax.ShapeDtypeStruct((b, H_FOX, HEAD_DIM, s), BF16),
        scratch_shapes=[pltpu.VMEM((2, nc, tk, tq), F32)],
        compiler_params=_cparams(("parallel", "parallel", "arbitrary")),
        name=kind + "_attn",
    )(*args)


def _dilated_kernel(*refs, tl, has_prev, is_last):
    it = iter(refs)
    q_ref, km_ref, kh_ref, vm_ref, vh_ref, bm_ref = (next(it) for _ in range(6))
    po_ref = pl_ref = g_ref = None
    if has_prev:
        po_ref, pl_ref = next(it), next(it)
    if is_last:
        g_ref = next(it)
    o_ref = next(it)
    lse_ref = None if is_last else next(it)

    first = pl.program_id(2) == 0
    lane = lax.broadcasted_iota(jnp.int32, (1, LANES), 1)
    half_mask = [(lane < HEAD_DIM).astype(F32), (lane >= HEAD_DIM).astype(F32)]
    lane_q = lax.broadcasted_iota(jnp.int32, (BAND, LANES), 1)
    colk = lax.broadcasted_iota(jnp.int32, (BAND, 2 * BAND), 1)

    kf = jnp.concatenate([kh_ref[...], km_ref[...]], axis=0).astype(F32)
    vf = jnp.concatenate([vh_ref[...], vm_ref[...]], axis=0).astype(F32)

    for jj in range(tl // BAND):
        rows = slice(jj * BAND, (jj + 1) * BAND)
        krows = slice(jj * BAND, jj * BAND + 2 * BAND)
        lse_tile = jnp.zeros((BAND, LANES), F32)
        for s in range(H_DIL // 2):
            lanes = slice(LANES * s, LANES * (s + 1))
            qb = q_ref[rows, lanes]
            o_slab = jnp.zeros((BAND, LANES), F32)
            for par in range(2):
                h = 2 * s + par
                kk = (kf[krows, lanes] * half_mask[par]).astype(BF16)
                vv = (vf[krows, lanes] * half_mask[par]).astype(BF16)
                sc = _dot_nt(qb, kk) + bm_ref[h]
                if jj == 0:
                    sc = jnp.where(jnp.logical_and(first, colk < BAND), NEG, sc)
                m = jnp.max(sc, axis=-1, keepdims=True)
                pe = jnp.exp(sc - m)
                den = jnp.sum(pe, axis=-1, keepdims=True)
                o = _dot(pe.astype(BF16), vv) / den
                lse = m + jnp.log(den)
                if has_prev:
                    lp = pl_ref[rows, :][:, h:h + 1]
                    mm = jnp.maximum(lp, lse)
                    e1 = jnp.exp(lp - mm)
                    e2 = jnp.exp(lse - mm)
                    o = (e1 * (po_ref[rows, lanes] * half_mask[par]) + e2 * o) / (e1 + e2)
                    lse = mm + jnp.log(e1 + e2)
                if is_last:
                    ms = jnp.sum(o * o, axis=-1, keepdims=True) * (1.0 / HEAD_DIM)
                    o = o * lax.rsqrt(ms + EPS)
                else:
                    lse_tile = jnp.where(lane_q == h, lse, lse_tile)
                o_slab = o_slab + o
            if is_last:
                o_ref[rows, lanes] = (o_slab * g_ref[:, lanes]).astype(o_ref.dtype)
            else:
                o_ref[rows, lanes] = o_slab
        if not is_last:
            lse_ref[rows, :] = lse_tile


def _dilated(dil, prev, bm, gains, b, s, d, is_last):
    l = s // d
    tl = min(512, l)
    nsub = tl // BAND
    has_prev = prev is not None
    view = dil.reshape(b, l, d * 3 * W_DIL)
    main = lambda part: pl.BlockSpec((None, tl, W_DIL), lambda bi, r, i: (bi, i, 3 * r + part))
    halo = lambda part: pl.BlockSpec((None, BAND, W_DIL),
                                     lambda bi, r, i: (bi, jnp.maximum(i * nsub - 1, 0), 3 * r + part))
    ospec = pl.BlockSpec((None, tl, W_DIL), lambda bi, r, i: (bi, i, r))
    lspec = pl.BlockSpec((None, tl, LANES), lambda bi, r, i: (bi, i, r))
    in_specs = [main(0), main(1), halo(1), main(2), halo(2), pl.BlockSpec(bm.shape, lambda bi, r, i: (0, 0, 0))]
    args = [view, view, view, view, view, bm]
    if has_prev:
        in_specs += [ospec, lspec]
        args += [prev[0].reshape(b, l, d * W_DIL), prev[1].reshape(b, l, d * LANES)]
    if is_last:
        in_specs.append(pl.BlockSpec(gains.shape, lambda bi, r, i: (0, 0)))
        args.append(gains)
        out_specs = ospec
        out_shape = jax.ShapeDtypeStruct((b, l, d * W_DIL), BF16)
    else:
        out_specs = [ospec, lspec]
        out_shape = [jax.ShapeDtypeStruct((b, l, d * W_DIL), F32), jax.ShapeDtypeStruct((b, l, d * LANES), F32)]
    out = pl.pallas_call(
        functools.partial(_dilated_kernel, tl=tl, has_prev=has_prev, is_last=is_last),
        grid=(b, d, l // tl),
        in_specs=in_specs,
        out_specs=out_specs,
        out_shape=out_shape,
        compiler_params=_cparams(("parallel", "parallel", "arbitrary")),
        name=f"dilated_d{d}",
    )(*args)
    if is_last:
        return out.reshape(b, s, W_DIL)
    return out[0].reshape(b * s, W_DIL), out[1].reshape(b * s, LANES)


def _alibi_bias(d):
    slopes = 2.0 ** (-8.0 * jnp.arange(1, H_DIL + 1, dtype=F32) / H_DIL)
    dist = jnp.arange(BAND)[:, None] - jnp.arange(2 * BAND)[None, :] + BAND
    valid = (dist >= 0) & (dist <= BAND)
    bias = -slopes[:, None, None] * (dist * d).astype(F32)[None]
    return jnp.where(valid[None], bias, NEG)


def _dilated_all_kernel(q_ref, kc_ref, kp_ref, vc_ref, vp_ref, bm_ref, g_ref, o_ref, qf, kf, vf, of, lf, *, tp):
    first = pl.program_id(1) == 0
    n_slab = H_DIL // 2
    for s in range(n_slab):
        lanes = slice(LANES * s, LANES * (s + 1))
        qf[s] = q_ref[:, lanes].astype(F32)
        kf[s, 0:tp] = kp_ref[:, lanes].astype(F32)
        kf[s, tp:2 * tp] = kc_ref[:, lanes].astype(F32)
        vf[s, 0:tp] = vp_ref[:, lanes].astype(F32)
        vf[s, tp:2 * tp] = vc_ref[:, lanes].astype(F32)

    lane = lax.broadcasted_iota(jnp.int32, (1, LANES), 1)
    half_mask = [(lane < HEAD_DIM).astype(F32), (lane >= HEAD_DIM).astype(F32)]
    lane_q = lax.broadcasted_iota(jnp.int32, (BAND, LANES), 1)
    colk = lax.broadcasted_iota(jnp.int32, (BAND, 2 * BAND), 1)
    heads = [(s, par) for s in range(n_slab) for par in range(2)]

    for pi, (_, d) in enumerate(DIL_PATTERNS):
        has_prev = pi > 0
        is_last = pi == len(DIL_PATTERNS) - 1

        def block(it, carry, d=d, pi=pi, has_prev=has_prev, is_last=is_last):
            q0 = it % d + (it // d) * (d * BAND)
            k0 = tp + q0 - d * BAND
            no_halo = jnp.logical_and(first, it < d)
            rows_q = pl.ds(q0, BAND, stride=d)
            rows_k = pl.ds(k0, 2 * BAND, stride=d)
            scores, vals = {}, {}
            for s, par in heads:
                qb = qf[s, rows_q, :].astype(BF16)
                kk = (kf[s, rows_k, :] * half_mask[par]).astype(BF16)
                sc = _dot_nt(qb, kk) + bm_ref[pi, 2 * s + par]
                scores[s, par] = jnp.where(jnp.logical_and(no_halo, colk < BAND), NEG, sc)
                vals[s, par] = (vf[s, rows_k, :] * half_mask[par]).astype(BF16)
            lse_tile = jnp.zeros((BAND, LANES), F32)
            o_slab = [jnp.zeros((BAND, LANES), F32) for _ in range(n_slab)]
            for s, par in heads:
                h = 2 * s + par
                sc = scores[s, par]
                m = jnp.max(sc, axis=-1, keepdims=True)
                pe = jnp.exp(sc - m)
                den = jnp.sum(pe, axis=-1, keepdims=True)
                o = _dot(pe.astype(BF16), vals[s, par]) / den
                lse = m + jnp.log(den)
                if has_prev:
                    lp = lf[rows_q, :][:, h:h + 1]
                    mm = jnp.maximum(lp, lse)
                    e1 = jnp.exp(lp - mm)
                    e2 = jnp.exp(lse - mm)
                    o = (e1 * (of[s, rows_q, :] * half_mask[par]) + e2 * o) / (e1 + e2)
                    lse = mm + jnp.log(e1 + e2)
                if is_last:
                    ms = jnp.sum(o * o, axis=-1, keepdims=True) * (1.0 / HEAD_DIM)
                    o = o * lax.rsqrt(ms + EPS)
                else:
                    lse_tile = jnp.where(lane_q == h, lse, lse_tile)
                o_slab[s] = o_slab[s] + o
            for s in range(n_slab):
                of[s, rows_q, :] = o_slab[s]
            if not is_last:
                lf[rows_q, :] = lse_tile
            return carry

        lax.fori_loop(0, tp // BAND, block, 0)

    for s in range(n_slab):
        lanes = slice(LANES * s, LANES * (s + 1))
        o_ref[:, lanes] = (of[s] * g_ref[:, lanes]).astype(BF16)


def _dilated3_kernel(q_ref, k_ref, v_ref, bm_ref, g_ref, o_ref, qf, kf, vf, of, lf, kst, vst, *, tp):
    first = pl.program_id(1) == 0
    n_slab = H_DIL // 2
    for s in range(n_slab):
        lanes = slice(LANES * s, LANES * (s + 1))
        qf[s] = q_ref[:, lanes].astype(F32)

        @pl.when(first)
        def _(s=s):
            kf[s, 0:tp] = jnp.zeros((tp, LANES), F32)
            vf[s, 0:tp] = jnp.zeros((tp, LANES), F32)

        @pl.when(jnp.logical_not(first))
        def _(s=s):
            kf[s, 0:tp] = kf[s, tp:2 * tp]
            vf[s, 0:tp] = vf[s, tp:2 * tp]

        kf[s, tp:2 * tp] = k_ref[:, lanes].astype(F32)
        vf[s, tp:2 * tp] = v_ref[:, lanes].astype(F32)

    lane = lax.broadcasted_iota(jnp.int32, (1, LANES), 1)
    half_mask = [(lane < HEAD_DIM).astype(F32), (lane >= HEAD_DIM).astype(F32)]
    lane_q = lax.broadcasted_iota(jnp.int32, (BAND, LANES), 1)
    colk = lax.broadcasted_iota(jnp.int32, (BAND, 2 * BAND), 1)
    heads = [(s, par) for s in range(n_slab) for par in range(2)]

    for pi, (_, d) in enumerate(DIL_PATTERNS):
        has_prev = pi > 0
        is_last = pi == len(DIL_PATTERNS) - 1
        n = tp // d
        span = BAND + n
        for src, dst in ((kf, kst), (vf, vst)):
            for s in range(n_slab):
                for r in range(d):
                    halo = src[s, pl.ds(tp - d * BAND + r, BAND, stride=d), :]
                    cur = src[s, pl.ds(tp + r, n, stride=d), :]
                    for par in range(2):
                        dst[2 * s + par, r * span:r * span + BAND] = (halo * half_mask[par]).astype(BF16)
                        dst[2 * s + par, r * span + BAND:(r + 1) * span] = (cur * half_mask[par]).astype(BF16)

        def block(it, carry, d=d, pi=pi, has_prev=has_prev, is_last=is_last, span=span):
            r = it % d
            j = it // d
            q0 = r + j * (d * BAND)
            rows_q = pl.ds(q0, BAND, stride=d)
            rows_k = pl.ds(pl.multiple_of(r * span + j * BAND, BAND), 2 * BAND)
            no_halo = jnp.logical_and(first, j == 0)
            scores = {}
            for s, par in heads:
                qb = qf[s, rows_q, :].astype(BF16)
                sc = _dot_nt(qb, kst[2 * s + par, rows_k, :]) + bm_ref[pi, 2 * s + par]
                scores[s, par] = jnp.where(jnp.logical_and(no_halo, colk < BAND), NEG, sc)
            lse_tile = jnp.zeros((BAND, LANES), F32)
            o_slab = [jnp.zeros((BAND, LANES), F32) for _ in range(n_slab)]
            for s, par in heads:
                h = 2 * s + par
                sc = scores[s, par]
                m = jnp.max(sc, axis=-1, keepdims=True)
                pe = jnp.exp(sc - m)
                den = jnp.sum(pe, axis=-1, keepdims=True)
                o = _dot(pe.astype(BF16), vst[h, rows_k, :]) / den
                lse = m + jnp.log(den)
                if has_prev:
                    lp = lf[rows_q, :][:, h:h + 1]
                    mm = jnp.maximum(lp, lse)
                    e1 = jnp.exp(lp - mm)
                    e2 = jnp.exp(lse - mm)
                    o = (e1 * (of[s, rows_q, :] * half_mask[par]) + e2 * o) / (e1 + e2)
                    lse = mm + jnp.log(e1 + e2)
                if is_last:
                    ms = jnp.sum(o * o, axis=-1, keepdims=True) * (1.0 / HEAD_DIM)
                    o = o * lax.rsqrt(ms + EPS)
                else:
                    lse_tile = jnp.where(lane_q == h, lse, lse_tile)
                o_slab[s] = o_slab[s] + o
            for s in range(n_slab):
                of[s, rows_q, :] = o_slab[s]
            if not is_last:
                lf[rows_q, :] = lse_tile
            return carry

        lax.fori_loop(0, tp // BAND, block, 0)

    for s in range(n_slab):
        lanes = slice(LANES * s, LANES * (s + 1))
        o_ref[:, lanes] = (of[s] * g_ref[:, lanes]).astype(BF16)


def _dilated3(dil, bm, gains, b, s, tp):
    view = dil.reshape(b, s, 3 * W_DIL)
    part = lambda p: pl.BlockSpec((None, tp, W_DIL), lambda bi, i: (bi, i, p))
    n_slab = H_DIL // 2
    staged_rows = tp + BAND * DIL_PATTERNS[-1][1]
    return pl.pallas_call(
        functools.partial(_dilated3_kernel, tp=tp),
        grid=(b, s // tp),
        in_specs=[part(0), part(1), part(2),
                  pl.BlockSpec(bm.shape, lambda bi, i: (0, 0, 0, 0)), pl.BlockSpec(gains.shape, lambda bi, i: (0, 0))],
        out_specs=pl.BlockSpec((None, tp, W_DIL), lambda bi, i: (bi, i, 0)),
        out_shape=jax.ShapeDtypeStruct((b, s, W_DIL), BF16),
        scratch_shapes=[pltpu.VMEM((n_slab, tp, LANES), F32), pltpu.VMEM((n_slab, 2 * tp, LANES), F32),
                        pltpu.VMEM((n_slab, 2 * tp, LANES), F32), pltpu.VMEM((n_slab, tp, LANES), F32),
                        pltpu.VMEM((tp, LANES), F32),
                        pltpu.VMEM((H_DIL, staged_rows, LANES), BF16), pltpu.VMEM((H_DIL, staged_rows, LANES), BF16)],
        compiler_params=_cparams(("parallel", "arbitrary")),
        name="dilated",
    )(view, view, view, bm, gains)


def _dilated_all(dil, bm, gains, b, s, tp):
    view = dil.reshape(b, s, 3 * W_DIL)
    cur = lambda part: pl.BlockSpec((None, tp, W_DIL), lambda bi, i: (bi, i, part))
    prev = lambda part: pl.BlockSpec((None, tp, W_DIL), lambda bi, i: (bi, jnp.maximum(i - 1, 0), part))
    n_slab = H_DIL // 2
    return pl.pallas_call(
        functools.partial(_dilated_all_kernel, tp=tp),
        grid=(b, s // tp),
        in_specs=[cur(0), cur(1), prev(1), cur(2), prev(2),
                  pl.BlockSpec(bm.shape, lambda bi, i: (0, 0, 0, 0)), pl.BlockSpec(gains.shape, lambda bi, i: (0, 0))],
        out_specs=pl.BlockSpec((None, tp, W_DIL), lambda bi, i: (bi, i, 0)),
        out_shape=jax.ShapeDtypeStruct((b, s, W_DIL), BF16),
        scratch_shapes=[pltpu.VMEM((n_slab, tp, LANES), F32), pltpu.VMEM((n_slab, 2 * tp, LANES), F32),
                        pltpu.VMEM((n_slab, 2 * tp, LANES), F32), pltpu.VMEM((n_slab, tp, LANES), F32),
                        pltpu.VMEM((tp, LANES), F32)],
        compiler_params=_cparams(("parallel", "arbitrary")),
        name="dilated",
    )(view, view, view, view, view, bm, gains)


def _outproj_kernel(h_ref, od_ref, of_ref, os_ref, wd_ref, wf_ref, ws_ref, o_ref):
    o_ref[...] = (h_ref[...] + _dot(od_ref[...], wd_ref[...]) + _dot_tn(of_ref[...], wf_ref[...])
                  + _dot_tn(os_ref[...], ws_ref[...]))


def _outproj(h, od, of, os_, wd, wf, ws, b, s, tm):
    full = lambda a: pl.BlockSpec(a.shape, lambda bi, i: (0, 0))
    tile = lambda w: pl.BlockSpec((None, tm, w), lambda bi, i: (bi, i, 0))
    tile_t = pl.BlockSpec((None, W_FOX, tm), lambda bi, i: (bi, 0, i))
    return pl.pallas_call(
        _outproj_kernel,
        grid=(b, s // tm),
        in_specs=[tile(D_MODEL), tile(W_DIL), tile_t, tile_t, full(wd), full(wf), full(ws)],
        out_specs=tile(D_MODEL),
        out_shape=jax.ShapeDtypeStruct((b, s, D_MODEL), F32),
        compiler_params=_cparams(("parallel", "parallel")),
        name="outproj",
    )(h.reshape(b, s, D_MODEL), od, of, os_, wd, wf, ws).reshape(b * s, D_MODEL)


def _swiglu_tile(u, wg_ref, wu_ref, wd_ref, hid_ref):
    for c in range(0, D_FF, 256):
        g = _dot(u, wg_ref[:, c:c + 256])
        v = _dot(u, wu_ref[:, c:c + 256])
        hid_ref[:, c:c + 256] = (g * (1.0 / (1.0 + jnp.exp(-g))) * v).astype(BF16)
    return _dot(hid_ref[...], wd_ref[...])


def _ffn_kernel(x_ref, g_ref, wg_ref, wu_ref, wd_ref, o_ref, u_ref, hid_ref):
    x = x_ref[...]
    u_ref[...] = _rms(x, g_ref[...]).astype(BF16)
    o_ref[...] = x + _swiglu_tile(u_ref[...], wg_ref, wu_ref, wd_ref, hid_ref)


def _ffn(h, g, wg, wu, wd, tm):
    t = h.shape[0]
    full = lambda a: pl.BlockSpec(a.shape, lambda i: (0, 0))
    tile = pl.BlockSpec((tm, D_MODEL), lambda i: (i, 0))
    return pl.pallas_call(
        _ffn_kernel,
        grid=(t // tm,),
        in_specs=[tile, full(g), full(wg), full(wu), full(wd)],
        out_specs=tile,
        out_shape=jax.ShapeDtypeStruct((t, D_MODEL), F32),
        scratch_shapes=[pltpu.VMEM((tm, D_MODEL), BF16), pltpu.VMEM((tm, D_FF), BF16)],
        compiler_params=_cparams(("parallel",)),
        name="ffn",
    )(h, g, wg, wu, wd)


R_I1, R_I2, R_G1, R_G2, R_K1, R_K2 = 8, 9, 10, 11, 12, 13


def _router_kernel(x_ref, g_ref, rw_ref, route_ref, cnt_ref, carry_ref, *, tm):
    @pl.when(pl.program_id(0) == 0)
    def _():
        carry_ref[...] = jnp.zeros_like(carry_ref)

    lane = lax.broadcasted_iota(jnp.int32, (tm, LANES), 1)
    u = _rms(x_ref[...], g_ref[...])
    logits = jnp.dot(u, rw_ref[...], preferred_element_type=F32, precision=lax.Precision.HIGHEST)
    logits = jnp.where(lane < N_EXPERTS, logits, NEG)
    t1 = jnp.max(logits, axis=-1, keepdims=True)
    i1 = jnp.min(jnp.where(logits == t1, lane, LANES), axis=-1, keepdims=True)
    rest = jnp.where(lane == i1, NEG, logits)
    t2 = jnp.max(rest, axis=-1, keepdims=True)
    i2 = jnp.min(jnp.where(rest == t2, lane, LANES), axis=-1, keepdims=True)
    e2 = jnp.exp(t2 - t1)
    g1 = 1.0 / (1.0 + e2)
    g2 = e2 / (1.0 + e2)

    onehot = jnp.where(lane == i1, 1.0, 0.0) + jnp.where(lane == i2, 1.0, 0.0)
    row = lax.broadcasted_iota(jnp.int32, (tm, tm), 0)
    col = lax.broadcasted_iota(jnp.int32, (tm, tm), 1)
    strict = jnp.where(col < row, 1.0, 0.0).astype(BF16)
    before = _dot(strict, onehot.astype(BF16)) + carry_ref[0:1, :]
    k1 = jnp.sum(jnp.where(lane == i1, before, 0.0), axis=-1, keepdims=True)
    k2 = jnp.sum(jnp.where(lane == i2, before, 0.0), axis=-1, keepdims=True)
    carry_ref[0:1, :] = before[tm - 1:tm, :] + onehot[tm - 1:tm, :]

    rec = jnp.zeros((tm, LANES), F32)
    for ln, val in ((R_I1, i1.astype(F32)), (R_I2, i2.astype(F32)), (R_G1, g1), (R_G2, g2), (R_K1, k1), (R_K2, k2)):
        rec = jnp.where(lane == ln, val, rec)
    route_ref[...] = rec
    cnt_ref[...] = jnp.broadcast_to(carry_ref[0:1, :], cnt_ref.shape)


def _router(h, g, rw, tm):
    t = h.shape[0]
    full = lambda a: pl.BlockSpec(a.shape, lambda i: (0, 0))
    return pl.pallas_call(
        functools.partial(_router_kernel, tm=tm),
        grid=(t // tm,),
        in_specs=[pl.BlockSpec((tm, D_MODEL), lambda i: (i, 0)), full(g), full(rw)],
        out_specs=[pl.BlockSpec((tm, LANES), lambda i: (i, 0)), pl.BlockSpec((8, LANES), lambda i: (0, 0))],
        out_shape=[jax.ShapeDtypeStruct((t, LANES), F32), jax.ShapeDtypeStruct((8, LANES), F32)],
        scratch_shapes=[pltpu.VMEM((8, LANES), F32)],
        compiler_params=_cparams(("arbitrary",)),
        name="router",
    )(h, g, rw)


def _dispatch_kernel(pos_ref, x_ref, g_ref, xs_in_ref, xs_ref, ubuf, sem, *, tm):
    del xs_in_ref
    i = pl.program_id(0)
    slot = i % 2
    ubuf[slot] = _rms(x_ref[...], g_ref[...])

    def issue(r, carry):
        for k in range(2):
            dst = pos_ref[0, k * tm + r]
            pltpu.make_async_copy(ubuf.at[slot, pl.ds(r, 1), :], xs_ref.at[pl.ds(dst, 1), :], sem.at[slot]).start()
        return carry

    lax.fori_loop(0, tm, issue, 0)

    def wait_slot(s):
        for _ in range(2):
            pltpu.make_async_copy(ubuf.at[s], ubuf.at[s], sem.at[s]).wait()

    @pl.when(i > 0)
    def _():
        wait_slot(1 - slot)

    @pl.when(i == pl.num_programs(0) - 1)
    def _():
        wait_slot(slot)


def _dispatch(h, g, pos_tiles, xs_zero, tm):
    t = h.shape[0]
    return pl.pallas_call(
        functools.partial(_dispatch_kernel, tm=tm),
        grid=(t // tm,),
        in_specs=[pl.BlockSpec((None, 1, 2 * tm), lambda i: (i, 0, 0), memory_space=pltpu.SMEM),
                  pl.BlockSpec((tm, D_MODEL), lambda i: (i, 0)),
                  pl.BlockSpec(g.shape, lambda i: (0, 0)),
                  pl.BlockSpec(memory_space=pl.ANY)],
        out_specs=pl.BlockSpec(memory_space=pl.ANY),
        out_shape=jax.ShapeDtypeStruct(xs_zero.shape, F32),
        scratch_shapes=[pltpu.VMEM((2, tm, D_MODEL), F32), pltpu.SemaphoreType.DMA((2,))],
        input_output_aliases={3: 0},
        compiler_params=_cparams(("arbitrary",)),
        name="dispatch",
    )(pos_tiles, h, g, xs_zero)


def _experts_kernel(te_ref, nreal_ref, x_ref, wg_ref, wu_ref, wd_ref, y_ref, hid_ref):
    del te_ref
    used = pl.program_id(0) < nreal_ref[0]

    @pl.when(used)
    def _():
        y_ref[...] = _swiglu_tile(x_ref[...].astype(BF16), wg_ref, wu_ref, wd_ref, hid_ref)

    @pl.when(jnp.logical_not(used))
    def _():
        y_ref[...] = jnp.zeros_like(y_ref)


def _experts(xs, tile_expert, n_real, wg, wu, wd, tm):
    p = xs.shape[0]
    rows = lambda i, te, nr: (jnp.minimum(i, nr[0] - 1), 0)
    expert = lambda a: pl.BlockSpec((None,) + a.shape[1:], lambda i, te, nr: (te[i], 0, 0))
    return pl.pallas_call(
        _experts_kernel,
        grid_spec=pltpu.PrefetchScalarGridSpec(
            num_scalar_prefetch=2,
            grid=(p // tm,),
            in_specs=[pl.BlockSpec((tm, D_MODEL), rows), expert(wg), expert(wu), expert(wd)],
            out_specs=pl.BlockSpec((tm, D_MODEL), lambda i, te, nr: (i, 0)),
            scratch_shapes=[pltpu.VMEM((tm, D_FF), BF16)]),
        out_shape=jax.ShapeDtypeStruct((p, D_MODEL), F32),
        compiler_params=_cparams(("arbitrary",)),
        name="experts",
    )(tile_expert, n_real, xs, wg, wu, wd)


def _combine_kernel(pos_ref, posn_ref, h_ref, route_ref, fg_ref, ys_ref, o_ref, ybuf, sem, *, tm, final_norm):
    i = pl.program_id(0)
    slot = i % 2

    def gather(p_ref, s):
        def issue(r, carry):
            pltpu.make_async_copy(ys_ref.at[pl.ds(p_ref[0, r], 1), :], ybuf.at[s, pl.ds(r, 1), :], sem.at[s]).start()
            return carry
        lax.fori_loop(0, 2 * tm, issue, 0)

    @pl.when(i == 0)
    def _():
        gather(pos_ref, slot)

    @pl.when(i + 1 < pl.num_programs(0))
    def _():
        gather(posn_ref, 1 - slot)

    pltpu.make_async_copy(ybuf.at[slot], ybuf.at[slot], sem.at[slot]).wait()
    route = route_ref[...]
    y = route[:, R_G1:R_G1 + 1] * ybuf[slot, 0:tm, :] + route[:, R_G2:R_G2 + 1] * ybuf[slot, tm:2 * tm, :]
    res = h_ref[...] + y
    if final_norm:
        res = _rms(res, fg_ref[...])
    o_ref[...] = res


def _combine(h, route, fg, pos_tiles, ys, tm, final_norm):
    t = h.shape[0]
    n = t // tm
    tile = lambda w: pl.BlockSpec((tm, w), lambda i: (i, 0))
    smem = lambda f: pl.BlockSpec((None, 1, 2 * tm), lambda i: (f(i), 0, 0), memory_space=pltpu.SMEM)
    return pl.pallas_call(
        functools.partial(_combine_kernel, tm=tm, final_norm=final_norm),
        grid=(n,),
        in_specs=[smem(lambda i: i), smem(lambda i: jnp.minimum(i + 1, n - 1)), tile(D_MODEL), tile(LANES),
                  pl.BlockSpec(fg.shape, lambda i: (0, 0)), pl.BlockSpec(memory_space=pl.ANY)],
        out_specs=tile(D_MODEL),
        out_shape=jax.ShapeDtypeStruct((t, D_MODEL), F32),
        scratch_shapes=[pltpu.VMEM((2, 2 * tm, D_MODEL), F32), pltpu.SemaphoreType.DMA((2,))],
        compiler_params=_cparams(("arbitrary",)),
        name="combine",
    )(pos_tiles, pos_tiles, h, route, fg, ys)


def _moe(h, g, rw, wg, wu, wd, fg, tm, final_norm):
    t = h.shape[0]
    n_tiles = 2 * t // tm + N_EXPERTS
    route, cnt = _router(h, g, rw, tm)
    counts = cnt[0, :N_EXPERTS].astype(jnp.int32)
    padded = (counts + tm - 1) // tm * tm
    ends = jnp.cumsum(padded)
    starts = ends - padded
    chosen = route[:, R_I1:R_I2 + 1].astype(jnp.int32)
    pos = jnp.sum(jnp.where(chosen[..., None] == jnp.arange(N_EXPERTS), starts, 0), axis=-1)
    pos = pos + route[:, R_K1:R_K2 + 1].astype(jnp.int32)
    pos_tiles = pos.reshape(t // tm, tm, 2).transpose(0, 2, 1).reshape(t // tm, 1, 2 * tm)
    n_real = ends[-1:] // tm
    tile_ids = jnp.minimum(jnp.arange(n_tiles), n_real[0] - 1)
    tile_expert = jnp.sum(tile_ids[:, None] * tm >= ends[None, :], axis=-1).astype(jnp.int32)

    xs = _dispatch(h, g, pos_tiles, jnp.zeros((n_tiles * tm, D_MODEL), F32), tm)
    ys = _experts(xs, tile_expert, n_real.astype(jnp.int32), wg, wu, wd, tm)
    return _combine(h, route, fg, pos_tiles, ys, tm, final_norm)


def _rmsnorm_kernel(x_ref, g_ref, o_ref):
    o_ref[...] = _rms(x_ref[...], g_ref[...])


def _rmsnorm(h, g, tm):
    t = h.shape[0]
    tile = pl.BlockSpec((tm, D_MODEL), lambda i: (i, 0))
    return pl.pallas_call(
        _rmsnorm_kernel, grid=(t // tm,),
        in_specs=[tile, pl.BlockSpec(g.shape, lambda i: (0, 0))], out_specs=tile,
        out_shape=jax.ShapeDtypeStruct((t, D_MODEL), F32),
        compiler_params=_cparams(("parallel",)), name="final_norm",
    )(h, g)


def _mixer_weights(w_in, forget_bias, head_norm, w_out, tq):
    d = D_MODEL
    a, b = W_DIL, W_DIL + W_FOX
    wq, wk, wv, wl = w_in[:, :d] * 0.125, w_in[:, d:2 * d], w_in[:, 2 * d:3 * d], w_in[:, 3 * d:]
    wd = jnp.concatenate([wq[:, :a], wk[:, :a], wv[:, :a]], axis=1).astype(BF16)
    wf = jnp.concatenate([wv[:, a:b], wv[:, b:], wq[:, a:b], wk[:, a:b], wq[:, b:], wk[:, b:]], axis=1).astype(BF16)
    wl = jnp.pad(wl, ((0, 0), (0, LANES - H_FOX))).astype(BF16)
    fb = jnp.pad(forget_bias, (0, LANES - H_FOX)).reshape(1, LANES)
    gcol = lambda g: jnp.broadcast_to(g.reshape(H_FOX, HEAD_DIM, 1), (H_FOX, HEAD_DIM, tq))
    return dict(wd=wd, wf=wf, wl=wl, fb=fb, wo_d=w_out[:a].astype(BF16), wo_f=w_out[a:b].astype(BF16),
                wo_s=w_out[b:].astype(BF16),
                g_dil=head_norm[:a].reshape(1, a), g_fox=gcol(head_norm[a:b]), g_sb=gcol(head_norm[b:]))


def _token_mixer(h, norm_g, w, bias_mats, b, s, tm, nc, tq, tk):
    dil, fs, lf = _inproj(h, norm_g, w["wd"], w["wf"], w["wl"], w["fb"], tm)
    q, k, vt = _prep(fs, lf, b, s, min(512, s), tk)
    o_dil = _dilated_all(dil, bias_mats,w["g_dil"], b, s, BAND * DIL_PATTERNS[-1][1])
    o_fox = _attn2(q, k, vt, w["g_fox"], "fox", nc, tq, tk).reshape(b, W_FOX, s)
    o_sb = _attn2(q, k, vt, w["g_sb"], "sb", nc, tq, tk).reshape(b, W_FOX, s)
    return _outproj(h, o_dil, o_fox, o_sb, w["wo_d"], w["wo_f"], w["wo_s"], b, s, tm)


def kernel(x, mix_norm, w_in, forget_bias, head_norm, w_out, ffn_norm, dense_w_gate, dense_w_up, dense_w_down,
           router_w, moe_w_gate, moe_w_up, moe_w_down, final_norm):
    b, s, d = x.shape
    depth = mix_norm.shape[0]
    tm, nc, tq, tk = 512, 4, 512, 256
    h = x.reshape(b * s, d)
    bias_mats = jnp.stack([_alibi_bias(dd) for _, dd in DIL_PATTERNS])
    fg = final_norm.reshape(1, d)
    for layer in range(depth):
        w = _mixer_weights(w_in[layer], forget_bias[layer], head_norm[layer], w_out[layer], tq)
        h = _token_mixer(h, mix_norm[layer].reshape(1, d), w, bias_mats, b, s, tm, nc, tq, tk)
        g = ffn_norm[layer].reshape(1, d)
        i = layer // 2
        if layer % 2 == 0:
            h = _ffn(h, g, dense_w_gate[i].astype(BF16), dense_w_up[i].astype(BF16), dense_w_down[i].astype(BF16), tm)
        else:
            rw = jnp.pad(router_w[i], ((0, 0), (0, LANES - N_EXPERTS)))
            h = _moe(h, g, rw, moe_w_gate[i].astype(BF16), moe_w_up[i].astype(BF16), moe_w_down[i].astype(BF16),
                     fg, tm, final_norm=(layer == depth - 1))
    if depth % 2 == 1:
        h = _rmsnorm(h, fg, tm)
    return h.reshape(b, s, d)
```

```python
import functools

import jax
import jax.numpy as jnp
from jax import lax
from jax.experimental import pallas as pl
from jax.experimental.pallas import tpu as pltpu

D_MODEL = 1024
HEAD_DIM = 64
H_DIL, H_FOX, H_SB = 6, 5, 5
DIL_PATTERNS = ((128, 1), (512, 4), (2048, 16))
BAND = 128
D_FF = 2816
N_EXPERTS = 8
EPS = 1e-6
NEG = -1e30
LANES = 128
W_DIL = H_DIL * HEAD_DIM
W_FOX = H_FOX * HEAD_DIM
W_FS = 3 * (H_FOX + H_SB) * HEAD_DIM
VMEM_LIMIT = 56 * 1024 * 1024

F32 = jnp.float32
BF16 = jnp.bfloat16
NT = (((1,), (1,)), ((), ()))
TN = (((0,), (0,)), ((), ()))


def _dot(a, b):
    return jnp.dot(a, b, preferred_element_type=F32)


def _dot_nt(a, b):
    return lax.dot_general(a, b, NT, preferred_element_type=F32)


def _dot_tn(a, b):
    return lax.dot_general(a, b, TN, preferred_element_type=F32)


def _rms(x, g):
    return x * lax.rsqrt(jnp.mean(x * x, axis=-1, keepdims=True) + EPS) * g


def _split3(x):
    x1 = x.astype(BF16)
    r = x - x1.astype(F32)
    x2 = r.astype(BF16)
    x3 = (r - x2.astype(F32)).astype(BF16)
    return x1, x2, x3


def _cparams(sem):
    return pltpu.CompilerParams(dimension_semantics=sem, vmem_limit_bytes=VMEM_LIMIT)


def _inproj_kernel(x_ref, g_ref, wd_ref, wf_ref, wl_ref, fb_ref, dil_ref, fs_ref, lf_ref, u_ref):
    u_ref[...] = _rms(x_ref[...], g_ref[...]).astype(BF16)
    u = u_ref[...]
    for w_ref, o_ref in ((wd_ref, dil_ref), (wf_ref, fs_ref)):
        n = w_ref.shape[1]
        for c in range(0, n, 512):
            e = min(c + 512, n)
            o_ref[:, c:e] = _dot(u, w_ref[:, c:e]).astype(BF16)
    z = _dot(u, wl_ref[...]) + fb_ref[...]
    lf_ref[...] = jnp.minimum(z, 0.0) - jnp.log1p(jnp.exp(-jnp.abs(z)))


def _inproj(h, g, wd, wf, wl, fb, tm):
    t = h.shape[0]
    full = lambda a: pl.BlockSpec(a.shape, lambda i: (0, 0))
    return pl.pallas_call(
        _inproj_kernel,
        grid=(t // tm,),
        in_specs=[pl.BlockSpec((tm, D_MODEL), lambda i: (i, 0)), full(g), full(wd), full(wf), full(wl), full(fb)],
        out_specs=[pl.BlockSpec((tm, 3 * W_DIL), lambda i: (i, 0)),
                   pl.BlockSpec((tm, W_FS), lambda i: (i, 0)),
                   pl.BlockSpec((tm, LANES), lambda i: (i, 0))],
        out_shape=[jax.ShapeDtypeStruct((t, 3 * W_DIL), BF16),
                   jax.ShapeDtypeStruct((t, W_FS), BF16),
                   jax.ShapeDtypeStruct((t, LANES), F32)],
        scratch_shapes=[pltpu.VMEM((tm, D_MODEL), BF16)],
        compiler_params=_cparams(("parallel",)),
        name="inproj",
    )(h, g, wd, wf, wl, fb)


def _prep_kernel(fs_ref, lf_ref, q_ref, k_ref, vt_ref, carry_ref, *, tc, tk):
    @pl.when(pl.program_id(1) == 0)
    def _():
        carry_ref[...] = jnp.zeros_like(carry_ref)

    row = lax.broadcasted_iota(jnp.int32, (tc, tc), 0)
    col = lax.broadcasted_iota(jnp.int32, (tc, tc), 1)
    tri = jnp.where(col <= row, 1.0, 0.0).astype(BF16)
    x1, x2, x3 = _split3(lf_ref[...])
    c = _dot(tri, x1) + _dot(tri, x2) + _dot(tri, x3) + carry_ref[0:1, :]
    carry_ref[0:1, :] = c[tc - 1:tc, :]

    n_heads = H_FOX + H_SB
    for s in range(n_heads // 2):
        vt = fs_ref[:, LANES * s:LANES * (s + 1)].astype(F32).T
        for half in range(2):
            for ci in range(tc // tk):
                vt_ref[2 * s + half, ci] = vt[HEAD_DIM * half:HEAD_DIM * (half + 1), ci * tk:(ci + 1) * tk].astype(BF16)

    lane = lax.broadcasted_iota(jnp.int32, (tc, LANES), 1)
    low = lane < HEAD_DIM

    def head(col0):
        s = col0 // LANES
        x = fs_ref[:, LANES * s:LANES * (s + 1)].astype(F32)
        if (col0 // HEAD_DIM) % 2 == 1:
            x = pltpu.roll(x, HEAD_DIM, 1)
        return jnp.where(low, x, 0.0)

    base = n_heads * HEAD_DIM
    for j in range(H_FOX):
        cj = c[:, j:j + 1]
        c1 = cj.astype(BF16).astype(F32)
        r = cj - c1
        c2 = r.astype(BF16).astype(F32)
        c3 = r - c2
        qa = head(base + HEAD_DIM * j)
        ka = head(base + HEAD_DIM * (H_FOX + j))
        for off, (qv, kv) in enumerate(((c1, 1.0), (c2, 1.0), (c3, 1.0), (1.0, -c1), (1.0, -c2), (1.0, -c3))):
            sel = lane == HEAD_DIM + off
            qa = jnp.where(sel, qv, qa)
            ka = jnp.where(sel, kv, ka)
        q_ref[j] = qa.astype(BF16)
        k_ref[j] = ka.astype(BF16)
    base_sb = base + 2 * H_FOX * HEAD_DIM
    for j in range(H_SB):
        q_ref[H_FOX + j] = head(base_sb + HEAD_DIM * j).astype(BF16)
        k_ref[H_FOX + j] = head(base_sb + HEAD_DIM * (H_SB + j)).astype(BF16)


def _prep(fs, lf, b, s, tc, tk):
    nh = H_FOX + H_SB
    nk = s // tk
    return pl.pallas_call(
        functools.partial(_prep_kernel, tc=tc, tk=tk),
        grid=(b, s // tc),
        in_specs=[pl.BlockSpec((None, tc, W_FS), lambda bi, si: (bi, si, 0)),
                  pl.BlockSpec((None, tc, LANES), lambda bi, si: (bi, si, 0))],
        out_specs=[pl.BlockSpec((None, nh, tc, LANES), lambda bi, si: (bi, 0, si, 0)),
                   pl.BlockSpec((None, nh, tc, LANES), lambda bi, si: (bi, 0, si, 0)),
                   pl.BlockSpec((None, nh, tc // tk, HEAD_DIM, tk), lambda bi, si: (bi, 0, si, 0, 0))],
        out_shape=[jax.ShapeDtypeStruct((b, nh, s, LANES), BF16),
                   jax.ShapeDtypeStruct((b, nh, s, LANES), BF16),
                   jax.ShapeDtypeStruct((b, nh, nk, HEAD_DIM, tk), BF16)],
        scratch_shapes=[pltpu.VMEM((8, LANES), F32)],
        compiler_params=_cparams(("parallel", "arbitrary")),
        name="prep",
    )(fs.reshape(b, s, W_FS), lf.reshape(b, s, LANES))


LOG2E = 1.4426950408889634


def _neg_abs(x):
    return pltpu.bitcast(pltpu.bitcast(x, jnp.uint32) | jnp.uint32(0x80000000), F32)


def _attn_kernel(*refs, kind, nc, tq, tk):
    if kind == "sb":
        q_ref, k_ref, v_ref, g_ref, up_ref, o_ref, sbuf = refs
        up2 = up_ref[...]
    else:
        q_ref, k_ref, v_ref, g_ref, o_ref, sbuf = refs
    r = tq // tk
    nt = nc * r
    nfull = pl.program_id(2) * nt
    row = lax.broadcasted_iota(jnp.int32, (tk, tq), 0)
    col = lax.broadcasted_iota(jnp.int32, (tk, tq), 1)
    qs = [q_ref[c * tq:(c + 1) * tq, :] for c in range(nc)]
    every = list(range(nc))
    live_at = lambda jj: [c for c in every if jj < (c + 1) * r]

    def scores_of(j, live):
        kb = k_ref[pl.ds(pl.multiple_of(j * tk, tk), tk), :]
        return {c: _dot_nt(kb, qs[c]) for c in live}

    def park(scores, slot):
        for c, sc in scores.items():
            sbuf[slot, c] = sc

    def parked(slot):
        return {c: sbuf[slot, c] for c in every}

    def fox_update(scores, vt, carries, rels):
        out = {}
        for c, st in scores.items():
            m, l, acc = carries[c]
            if rels[c] is not None:
                st = jnp.where(row + rels[c] <= col, st, NEG)
            m_new = jnp.maximum(m, jnp.max(st, axis=0, keepdims=True))
            alpha = jnp.exp(m - m_new)
            pt = jnp.exp(st - m_new)
            out[c] = (m_new, alpha * l + jnp.sum(pt, axis=0, keepdims=True), alpha * acc + _dot(vt, pt.astype(BF16)))
        return out

    def sb_update(scores, vt, carries, rels):
        sps, laters, out = {}, {}, {}
        for c, z in scores.items():
            e = jnp.exp2(_neg_abs(z * LOG2E))
            sp = jnp.maximum(z, 0.0) + jnp.log(1.0 + e)
            if rels[c] is not None:
                sp = jnp.where(row + rels[c] < col, sp, 0.0)
            sps[c] = sp
        for c, sp in sps.items():
            hi = sp.astype(BF16)
            lo = (sp - hi.astype(F32)).astype(BF16)
            laters[c] = _dot(up2, jnp.concatenate([hi, lo], axis=0))
        for c, z in scores.items():
            rs, acc = carries[c]
            a = jnp.exp(z - sps[c] - laters[c])
            if rels[c] is not None:
                a = jnp.where(row + rels[c] < col, a, 0.0)
            out[c] = (rs + laters[c][0:1, :] + sps[c][0:1, :], acc + _dot(vt, a.astype(BF16)) * jnp.exp(-rs))
        return out

    def update(scores, j, carries, jj):
        rels = {c: None if jj is None or jj < c * r else jj * tk - c * tq for c in scores}
        new = (sb_update if kind == "sb" else fox_update)(scores, v_ref[j], carries, rels)
        return tuple(new.get(c, carries[c]) for c in every)

    zeros = jnp.zeros((HEAD_DIM, tq), F32)
    if kind == "sb":
        carries = tuple((jnp.zeros((1, tq), F32), zeros) for _ in every)
        cur = scores_of(nfull + nt - 1, live_at(nt - 1))
        for jj in reversed(range(nt)):
            if jj > 0:
                nxt = scores_of(nfull + jj - 1, live_at(jj - 1))
            else:
                park(scores_of(jnp.maximum(nfull - 1, 0), every), 0)
            carries = update(cur, nfull + jj, carries, jj)
            cur = nxt

        def pair(i2, cs):
            for half in range(2):
                j = nfull - 1 - (2 * i2 + half)
                park(scores_of(jnp.maximum(j - 1, 0), every), 1 - half)
                cs = update(parked(half), j, cs, None)
            return cs

        carries = lax.fori_loop(0, nfull // 2, pair, carries)
        outs = [acc for _, acc in carries]
    else:
        carries = tuple((jnp.full((1, tq), NEG, F32), jnp.zeros((1, tq), F32), zeros) for _ in every)
        park(scores_of(0, every), 0)

        def pair(i2, cs):
            for half in range(2):
                j = 2 * i2 + half
                park(scores_of(j + 1, every), 1 - half)
                cs = update(parked(half), j, cs, None)
            return cs

        carries = lax.fori_loop(0, nfull // 2, pair, carries)
        cur = parked(0)
        for jj in range(nt):
            if jj + 1 < nt:
                nxt = scores_of(nfull + jj + 1, live_at(jj + 1))
            carries = update(cur, nfull + jj, carries, jj)
            cur = nxt
        outs = [acc / l for _, l, acc in carries]

    g = g_ref[pl.program_id(1)]
    for c, o in enumerate(outs):
        ms = jnp.mean(o * o, axis=0, keepdims=True)
        o_ref[:, c * tq:(c + 1) * tq] = (o * lax.rsqrt(ms + EPS) * g).astype(BF16)


def _attn(q, k, vt, gains, kind, nc, tq, tk):
    b, _, s, _ = q.shape
    nk = s // tk
    tile = nc * tq
    assert (tile // tk) % 2 == 0
    base = H_FOX if kind == "sb" else 0
    in_specs = [pl.BlockSpec((None, None, tile, LANES), lambda bi, h, qi: (bi, base + h, qi, 0)),
                pl.BlockSpec((None, None, s, LANES), lambda bi, h, qi: (bi, base + h, 0, 0)),
                pl.BlockSpec((None, None, nk, HEAD_DIM, tk), lambda bi, h, qi: (bi, base + h, 0, 0, 0)),
                pl.BlockSpec(gains.shape, lambda bi, h, qi: (0, 0, 0))]
    args = [q, k, vt, gains]
    if kind == "sb":
        upper = jnp.triu(jnp.ones((tk, tk), F32), k=1).astype(BF16)
        in_specs.append(pl.BlockSpec((tk, 2 * tk), lambda bi, h, qi: (0, 0)))
        args.append(jnp.concatenate([upper, upper], axis=1))
    return pl.pallas_call(
        functools.partial(_attn_kernel, kind=kind, nc=nc, tq=tq, tk=tk),
        grid=(b, H_FOX, s // tile),
        in_specs=in_specs,
        out_specs=pl.BlockSpec((None, None, HEAD_DIM, tile), lambda bi, h, qi: (bi, h, 0, qi)),
        out_shape=jax.ShapeDtypeStruct((b, H_FOX, HEAD_DIM, s), BF16),
        scratch_shapes=[pltpu.VMEM((2, nc, tk, tq), F32)],
        compiler_params=_cparams(("parallel", "parallel", "arbitrary")),
        name=kind + "_attn",
    )(*args)


RUNS = DIL_PATTERNS[-1][1]
GROUP = RUNS * RUNS


def _runs_kernel(q_ref, k_ref, v_ref, bm_ref, g_ref, pm_ref, o_ref, qf, kf, vf, of, lf, *, tp):
    first = pl.program_id(1) == 0
    n_slab = H_DIL // 2
    span = tp // RUNS
    perm, perm_t = pm_ref[0], pm_ref[1]

    for s in range(n_slab):
        @pl.when(first)
        def _(s=s):
            kf[s] = jnp.zeros(kf.shape[1:], F32)
            vf[s] = jnp.zeros(vf.shape[1:], F32)

        @pl.when(jnp.logical_not(first))
        def _(s=s):
            for r in range(RUNS):
                kf[s, 2 * span * r:2 * span * r + span] = kf[s, 2 * span * r + span:2 * span * (r + 1)]
                vf[s, 2 * span * r:2 * span * r + span] = vf[s, 2 * span * r + span:2 * span * (r + 1)]

    for gi in range(tp // GROUP):
        rows = slice(GROUP * gi, GROUP * (gi + 1))
        for src, dst, stride, off in ((q_ref, qf, span, 0), (k_ref, kf, 2 * span, span), (v_ref, vf, 2 * span, span)):
            res = _dot(perm, src[rows, :])
            for s in range(n_slab):
                for r in range(RUNS):
                    at = stride * r + off + RUNS * gi
                    dst[s, at:at + RUNS] = res[RUNS * r:RUNS * (r + 1), LANES * s:LANES * (s + 1)]

    lane = lax.broadcasted_iota(jnp.int32, (1, LANES), 1)
    half_mask = [(lane < HEAD_DIM).astype(F32), (lane >= HEAD_DIM).astype(F32)]
    ones = jnp.ones((2 * BAND, LANES), BF16)
    heads = [(s, par) for s in range(n_slab) for par in range(2)]

    for pi, (_, d) in enumerate(DIL_PATTERNS):
        has_prev = pi > 0
        is_last = pi == len(DIL_PATTERNS) - 1
        pieces = RUNS // d
        lq = BAND // pieces

        def block(it, carry, d=d, pi=pi, has_prev=has_prev, is_last=is_last, pieces=pieces, lq=lq):
            m = it // d
            q_at = [pl.multiple_of(((it % d) + d * a) * span + lq * m, 8) for a in range(pieces)]
            k_at = [pl.multiple_of(((it % d) + d * a) * 2 * span + span - lq + lq * m, 8) for a in range(pieces)]
            gather = lambda ref, s, at, n: jnp.concatenate([ref[s, pl.ds(x, n), :] for x in at], axis=0)
            table = jnp.logical_and(first, m == 0).astype(jnp.int32)
            scores, vals = {}, {}
            for s in range(n_slab):
                qb = gather(qf, s, q_at, lq).astype(BF16)
                kcat = gather(kf, s, k_at, 2 * lq)
                vcat = gather(vf, s, k_at, 2 * lq)
                for par in range(2):
                    scores[s, par] = _dot_nt(qb, (kcat * half_mask[par]).astype(BF16)) + bm_ref[table, pi, 2 * s + par]
                    vals[s, par] = jnp.concatenate([(vcat * half_mask[par]).astype(BF16), ones], axis=1)
            o_slab = [jnp.zeros((BAND, LANES), F32) for _ in range(n_slab)]
            lses = {}
            for s, par in heads:
                h = 2 * s + par
                sc = scores[s, par]
                mx = jnp.max(sc, axis=-1, keepdims=True)
                pe = jnp.exp(sc - mx).astype(BF16)
                pv = _dot(pe, vals[s, par])
                den = pv[:, LANES:]
                o = pv[:, :LANES] / den
                lse = mx + jnp.log(den)
                if has_prev:
                    lp = jnp.concatenate([lf[h, pl.ds(x, lq), :] for x in q_at], axis=0)
                    mm = jnp.maximum(lp, lse)
                    e1 = jnp.exp(lp - mm)
                    e2 = jnp.exp(lse - mm)
                    o = (e1 * (gather(of, s, q_at, lq) * half_mask[par]) + e2 * o) / (e1 + e2)
                    lse = mm + jnp.log(e1 + e2)
                if is_last:
                    ms = jnp.sum(o * o, axis=-1, keepdims=True) * (1.0 / HEAD_DIM)
                    o = o * lax.rsqrt(ms + EPS)
                else:
                    lses[h] = lse
                o_slab[s] = o_slab[s] + o
            for a, x in enumerate(q_at):
                for s in range(n_slab):
                    of[s, pl.ds(x, lq), :] = o_slab[s][lq * a:lq * (a + 1)]
                for h, lse in lses.items():
                    lf[h, pl.ds(x, lq), :] = lse[lq * a:lq * (a + 1)]
            return carry

        lax.fori_loop(0, tp // BAND, block, 0)

    for s in range(n_slab):
        lanes = slice(LANES * s, LANES * (s + 1))
        y = (of[s] * g_ref[:, lanes]).astype(BF16)
        for gi in range(tp // GROUP):
            blk = jnp.concatenate([y[span * r + RUNS * gi:span * r + RUNS * (gi + 1)] for r in range(RUNS)], axis=0)
            o_ref[GROUP * gi:GROUP * (gi + 1), lanes] = _dot(perm_t, blk).astype(BF16)


def _runs_bias(d):
    pieces = RUNS // d
    lq = BAND // pieces
    slopes = 2.0 ** (-8.0 * jnp.arange(1, H_DIL + 1, dtype=F32) / H_DIL)
    i, j = jnp.arange(BAND), jnp.arange(2 * BAND)
    dist = (pieces * ((i % lq)[:, None] - (j % (2 * lq))[None, :] + lq) + (i // lq)[:, None] - (j // (2 * lq))[None, :])
    valid = (dist >= 0) & (dist <= BAND)
    own = (j % (2 * lq) >= lq)[None, :]
    bias = -slopes[:, None, None] * (dist * d).astype(F32)[None]
    return jnp.stack([jnp.where(valid[None], bias, NEG), jnp.where((valid & own)[None], bias, NEG)])


def _runs_perm():
    idx = jnp.arange(GROUP)
    p = (idx[None, :] == (idx[:, None] % RUNS) * RUNS + idx[:, None] // RUNS).astype(BF16)
    return jnp.stack([p, p.T])


def _dilated_runs(dil, bm, gains, perms, b, s, tp):
    assert tp == RUNS * BAND
    view = dil.reshape(b, s, 3 * W_DIL)
    part = lambda p: pl.BlockSpec((None, tp, W_DIL), lambda bi, i: (bi, i, p))
    n_slab = H_DIL // 2
    return pl.pallas_call(
        functools.partial(_runs_kernel, tp=tp),
        grid=(b, s // tp),
        in_specs=[part(0), part(1), part(2),
                  pl.BlockSpec(bm.shape, lambda bi, i: (0, 0, 0, 0, 0)), pl.BlockSpec(gains.shape, lambda bi, i: (0, 0)),
                  pl.BlockSpec(perms.shape, lambda bi, i: (0, 0, 0))],
        out_specs=pl.BlockSpec((None, tp, W_DIL), lambda bi, i: (bi, i, 0)),
        out_shape=jax.ShapeDtypeStruct((b, s, W_DIL), BF16),
        scratch_shapes=[pltpu.VMEM((n_slab, tp, LANES), F32), pltpu.VMEM((n_slab, 2 * tp, LANES), F32),
                        pltpu.VMEM((n_slab, 2 * tp, LANES), F32), pltpu.VMEM((n_slab, tp, LANES), F32),
                        pltpu.VMEM((H_DIL, tp, LANES), F32)],
        compiler_params=_cparams(("parallel", "arbitrary")),
        name="dilated",
    )(view, view, view, bm, gains, perms)


def _outproj_kernel(h_ref, od_ref, of_ref, os_ref, wd_ref, wf_ref, ws_ref, o_ref):
    o_ref[...] = (h_ref[...] + _dot(od_ref[...], wd_ref[...]) + _dot_tn(of_ref[...], wf_ref[...])
                  + _dot_tn(os_ref[...], ws_ref[...]))


def _outproj(h, od, of, os_, wd, wf, ws, b, s, tm):
    full = lambda a: pl.BlockSpec(a.shape, lambda bi, i: (0, 0))
    tile = lambda w: pl.BlockSpec((None, tm, w), lambda bi, i: (bi, i, 0))
    tile_t = pl.BlockSpec((None, W_FOX, tm), lambda bi, i: (bi, 0, i))
    return pl.pallas_call(
        _outproj_kernel,
        grid=(b, s // tm),
        in_specs=[tile(D_MODEL), tile(W_DIL), tile_t, tile_t, full(wd), full(wf), full(ws)],
        out_specs=tile(D_MODEL),
        out_shape=jax.ShapeDtypeStruct((b, s, D_MODEL), F32),
        compiler_params=_cparams(("parallel", "parallel")),
        name="outproj",
    )(h.reshape(b, s, D_MODEL), od, of, os_, wd, wf, ws).reshape(b * s, D_MODEL)


def _swiglu_tile(u, wg_ref, wu_ref, wd_ref, hid_ref):
    for c in range(0, D_FF, 256):
        g = _dot(u, wg_ref[:, c:c + 256])
        v = _dot(u, wu_ref[:, c:c + 256])
        hid_ref[:, c:c + 256] = (g * (1.0 / (1.0 + jnp.exp(-g))) * v).astype(BF16)
    return _dot(hid_ref[...], wd_ref[...])


def _ffn_kernel(x_ref, g_ref, wg_ref, wu_ref, wd_ref, o_ref, u_ref, hid_ref):
    x = x_ref[...]
    u_ref[...] = _rms(x, g_ref[...]).astype(BF16)
    o_ref[...] = x + _swiglu_tile(u_ref[...], wg_ref, wu_ref, wd_ref, hid_ref)


def _ffn(h, g, wg, wu, wd, tm):
    t = h.shape[0]
    full = lambda a: pl.BlockSpec(a.shape, lambda i: (0, 0))
    tile = pl.BlockSpec((tm, D_MODEL), lambda i: (i, 0))
    return pl.pallas_call(
        _ffn_kernel,
        grid=(t // tm,),
        in_specs=[tile, full(g), full(wg), full(wu), full(wd)],
        out_specs=tile,
        out_shape=jax.ShapeDtypeStruct((t, D_MODEL), F32),
        scratch_shapes=[pltpu.VMEM((tm, D_MODEL), BF16), pltpu.VMEM((tm, D_FF), BF16)],
        compiler_params=_cparams(("parallel",)),
        name="ffn",
    )(h, g, wg, wu, wd)


R_I1, R_I2, R_G1, R_G2, R_K1, R_K2 = 8, 9, 10, 11, 12, 13


def _router_kernel(x_ref, g_ref, rw_ref, route_ref, cnt_ref, carry_ref, *, tm):
    @pl.when(pl.program_id(0) == 0)
    def _():
        carry_ref[...] = jnp.zeros_like(carry_ref)

    lane = lax.broadcasted_iota(jnp.int32, (tm, LANES), 1)
    u = _rms(x_ref[...], g_ref[...])
    logits = jnp.dot(u, rw_ref[...], preferred_element_type=F32, precision=lax.Precision.HIGHEST)
    logits = jnp.where(lane < N_EXPERTS, logits, NEG)
    t1 = jnp.max(logits, axis=-1, keepdims=True)
    i1 = jnp.min(jnp.where(logits == t1, lane, LANES), axis=-1, keepdims=True)
    rest = jnp.where(lane == i1, NEG, logits)
    t2 = jnp.max(rest, axis=-1, keepdims=True)
    i2 = jnp.min(jnp.where(rest == t2, lane, LANES), axis=-1, keepdims=True)
    e2 = jnp.exp(t2 - t1)
    g1 = 1.0 / (1.0 + e2)
    g2 = e2 / (1.0 + e2)

    onehot = jnp.where(lane == i1, 1.0, 0.0) + jnp.where(lane == i2, 1.0, 0.0)
    row = lax.broadcasted_iota(jnp.int32, (tm, tm), 0)
    col = lax.broadcasted_iota(jnp.int32, (tm, tm), 1)
    strict = jnp.where(col < row, 1.0, 0.0).astype(BF16)
    before = _dot(strict, onehot.astype(BF16)) + carry_ref[0:1, :]
    k1 = jnp.sum(jnp.where(lane == i1, before, 0.0), axis=-1, keepdims=True)
    k2 = jnp.sum(jnp.where(lane == i2, before, 0.0), axis=-1, keepdims=True)
    carry_ref[0:1, :] = before[tm - 1:tm, :] + onehot[tm - 1:tm, :]

    rec = jnp.zeros((tm, LANES), F32)
    for ln, val in ((R_I1, i1.astype(F32)), (R_I2, i2.astype(F32)), (R_G1, g1), (R_G2, g2), (R_K1, k1), (R_K2, k2)):
        rec = jnp.where(lane == ln, val, rec)
    route_ref[...] = rec
    cnt_ref[...] = jnp.broadcast_to(carry_ref[0:1, :], cnt_ref.shape)


def _router(h, g, rw, tm):
    t = h.shape[0]
    full = lambda a: pl.BlockSpec(a.shape, lambda i: (0, 0))
    return pl.pallas_call(
        functools.partial(_router_kernel, tm=tm),
        grid=(t // tm,),
        in_specs=[pl.BlockSpec((tm, D_MODEL), lambda i: (i, 0)), full(g), full(rw)],
        out_specs=[pl.BlockSpec((tm, LANES), lambda i: (i, 0)), pl.BlockSpec((8, LANES), lambda i: (0, 0))],
        out_shape=[jax.ShapeDtypeStruct((t, LANES), F32), jax.ShapeDtypeStruct((8, LANES), F32)],
        scratch_shapes=[pltpu.VMEM((8, LANES), F32)],
        compiler_params=_cparams(("arbitrary",)),
        name="router",
    )(h, g, rw)


def _dispatch_kernel(pos_ref, x_ref, g_ref, xs_in_ref, xs_ref, ubuf, sem, *, tm):
    del xs_in_ref
    i = pl.program_id(0)
    slot = i % 2
    ubuf[slot] = _rms(x_ref[...], g_ref[...])

    def issue(r, carry):
        for k in range(2):
            dst = pos_ref[0, k * tm + r]
            pltpu.make_async_copy(ubuf.at[slot, pl.ds(r, 1), :], xs_ref.at[pl.ds(dst, 1), :], sem.at[slot]).start()
        return carry

    lax.fori_loop(0, tm, issue, 0)

    def wait_slot(s):
        for _ in range(2):
            pltpu.make_async_copy(ubuf.at[s], ubuf.at[s], sem.at[s]).wait()

    @pl.when(i > 0)
    def _():
        wait_slot(1 - slot)

    @pl.when(i == pl.num_programs(0) - 1)
    def _():
        wait_slot(slot)


def _dispatch(h, g, pos_tiles, xs_zero, tm):
    t = h.shape[0]
    return pl.pallas_call(
        functools.partial(_dispatch_kernel, tm=tm),
        grid=(t // tm,),
        in_specs=[pl.BlockSpec((None, 1, 2 * tm), lambda i: (i, 0, 0), memory_space=pltpu.SMEM),
                  pl.BlockSpec((tm, D_MODEL), lambda i: (i, 0)),
                  pl.BlockSpec(g.shape, lambda i: (0, 0)),
                  pl.BlockSpec(memory_space=pl.ANY)],
        out_specs=pl.BlockSpec(memory_space=pl.ANY),
        out_shape=jax.ShapeDtypeStruct(xs_zero.shape, F32),
        scratch_shapes=[pltpu.VMEM((2, tm, D_MODEL), F32), pltpu.SemaphoreType.DMA((2,))],
        input_output_aliases={3: 0},
        compiler_params=_cparams(("arbitrary",)),
        name="dispatch",
    )(pos_tiles, h, g, xs_zero)


def _experts_kernel(te_ref, nreal_ref, x_ref, wg_ref, wu_ref, wd_ref, y_ref, hid_ref):
    del te_ref
    used = pl.program_id(0) < nreal_ref[0]

    @pl.when(used)
    def _():
        y_ref[...] = _swiglu_tile(x_ref[...].astype(BF16), wg_ref, wu_ref, wd_ref, hid_ref)

    @pl.when(jnp.logical_not(used))
    def _():
        y_ref[...] = jnp.zeros_like(y_ref)


def _experts(xs, tile_expert, n_real, wg, wu, wd, tm):
    p = xs.shape[0]
    rows = lambda i, te, nr: (jnp.minimum(i, nr[0] - 1), 0)
    expert = lambda a: pl.BlockSpec((None,) + a.shape[1:], lambda i, te, nr: (te[i], 0, 0))
    return pl.pallas_call(
        _experts_kernel,
        grid_spec=pltpu.PrefetchScalarGridSpec(
            num_scalar_prefetch=2,
            grid=(p // tm,),
            in_specs=[pl.BlockSpec((tm, D_MODEL), rows), expert(wg), expert(wu), expert(wd)],
            out_specs=pl.BlockSpec((tm, D_MODEL), lambda i, te, nr: (i, 0)),
            scratch_shapes=[pltpu.VMEM((tm, D_FF), BF16)]),
        out_shape=jax.ShapeDtypeStruct((p, D_MODEL), F32),
        compiler_params=_cparams(("arbitrary",)),
        name="experts",
    )(tile_expert, n_real, xs, wg, wu, wd)


def _combine_kernel(pos_ref, posn_ref, h_ref, route_ref, fg_ref, ys_ref, o_ref, ybuf, sem, *, tm, final_norm):
    i = pl.program_id(0)
    slot = i % 2

    def gather(p_ref, s):
        def issue(r, carry):
            pltpu.make_async_copy(ys_ref.at[pl.ds(p_ref[0, r], 1), :], ybuf.at[s, pl.ds(r, 1), :], sem.at[s]).start()
            return carry
        lax.fori_loop(0, 2 * tm, issue, 0)

    @pl.when(i == 0)
    def _():
        gather(pos_ref, slot)

    @pl.when(i + 1 < pl.num_programs(0))
    def _():
        gather(posn_ref, 1 - slot)

    pltpu.make_async_copy(ybuf.at[slot], ybuf.at[slot], sem.at[slot]).wait()
    route = route_ref[...]
    y = route[:, R_G1:R_G1 + 1] * ybuf[slot, 0:tm, :] + route[:, R_G2:R_G2 + 1] * ybuf[slot, tm:2 * tm, :]
    res = h_ref[...] + y
    if final_norm:
        res = _rms(res, fg_ref[...])
    o_ref[...] = res


def _combine(h, route, fg, pos_tiles, ys, tm, final_norm):
    t = h.shape[0]
    n = t // tm
    tile = lambda w: pl.BlockSpec((tm, w), lambda i: (i, 0))
    smem = lambda f: pl.BlockSpec((None, 1, 2 * tm), lambda i: (f(i), 0, 0), memory_space=pltpu.SMEM)
    return pl.pallas_call(
        functools.partial(_combine_kernel, tm=tm, final_norm=final_norm),
        grid=(n,),
        in_specs=[smem(lambda i: i), smem(lambda i: jnp.minimum(i + 1, n - 1)), tile(D_MODEL), tile(LANES),
                  pl.BlockSpec(fg.shape, lambda i: (0, 0)), pl.BlockSpec(memory_space=pl.ANY)],
        out_specs=tile(D_MODEL),
        out_shape=jax.ShapeDtypeStruct((t, D_MODEL), F32),
        scratch_shapes=[pltpu.VMEM((2, 2 * tm, D_MODEL), F32), pltpu.SemaphoreType.DMA((2,))],
        compiler_params=_cparams(("arbitrary",)),
        name="combine",
    )(pos_tiles, pos_tiles, h, route, fg, ys)


def _moe(h, g, rw, wg, wu, wd, fg, tm, final_norm):
    t = h.shape[0]
    n_tiles = 2 * t // tm + N_EXPERTS
    route, cnt = _router(h, g, rw, tm)
    counts = cnt[0, :N_EXPERTS].astype(jnp.int32)
    padded = (counts + tm - 1) // tm * tm
    ends = jnp.cumsum(padded)
    starts = ends - padded
    chosen = route[:, R_I1:R_I2 + 1].astype(jnp.int32)
    pos = jnp.sum(jnp.where(chosen[..., None] == jnp.arange(N_EXPERTS), starts, 0), axis=-1)
    pos = pos + route[:, R_K1:R_K2 + 1].astype(jnp.int32)
    pos_tiles = pos.reshape(t // tm, tm, 2).transpose(0, 2, 1).reshape(t // tm, 1, 2 * tm)
    n_real = ends[-1:] // tm
    tile_ids = jnp.minimum(jnp.arange(n_tiles), n_real[0] - 1)
    tile_expert = jnp.sum(tile_ids[:, None] * tm >= ends[None, :], axis=-1).astype(jnp.int32)

    xs = _dispatch(h, g, pos_tiles, jnp.zeros((n_tiles * tm, D_MODEL), F32), tm)
    ys = _experts(xs, tile_expert, n_real.astype(jnp.int32), wg, wu, wd, tm)
    return _combine(h, route, fg, pos_tiles, ys, tm, final_norm)


def _rmsnorm_kernel(x_ref, g_ref, o_ref):
    o_ref[...] = _rms(x_ref[...], g_ref[...])


def _rmsnorm(h, g, tm):
    t = h.shape[0]
    tile = pl.BlockSpec((tm, D_MODEL), lambda i: (i, 0))
    return pl.pallas_call(
        _rmsnorm_kernel, grid=(t // tm,),
        in_specs=[tile, pl.BlockSpec(g.shape, lambda i: (0, 0))], out_specs=tile,
        out_shape=jax.ShapeDtypeStruct((t, D_MODEL), F32),
        compiler_params=_cparams(("parallel",)), name="final_norm",
    )(h, g)


def _mixer_weights(w_in, forget_bias, head_norm, w_out, tq):
    d = D_MODEL
    a, b = W_DIL, W_DIL + W_FOX
    wq, wk, wv, wl = w_in[:, :d] * 0.125, w_in[:, d:2 * d], w_in[:, 2 * d:3 * d], w_in[:, 3 * d:]
    wd = jnp.concatenate([wq[:, :a], wk[:, :a], wv[:, :a]], axis=1).astype(BF16)
    wf = jnp.concatenate([wv[:, a:b], wv[:, b:], wq[:, a:b], wk[:, a:b], wq[:, b:], wk[:, b:]], axis=1).astype(BF16)
    wl = jnp.pad(wl, ((0, 0), (0, LANES - H_FOX))).astype(BF16)
    fb = jnp.pad(forget_bias, (0, LANES - H_FOX)).reshape(1, LANES)
    gcol = lambda g: jnp.broadcast_to(g.reshape(H_FOX, HEAD_DIM, 1), (H_FOX, HEAD_DIM, tq))
    return dict(wd=wd, wf=wf, wl=wl, fb=fb, wo_d=w_out[:a].astype(BF16), wo_f=w_out[a:b].astype(BF16),
                wo_s=w_out[b:].astype(BF16),
                g_dil=head_norm[:a].reshape(1, a), g_fox=gcol(head_norm[a:b]), g_sb=gcol(head_norm[b:]))


def _token_mixer(h, norm_g, w, bias_mats, b, s, tm, nc, tq, tk):
    dil, fs, lf = _inproj(h, norm_g, w["wd"], w["wf"], w["wl"], w["fb"], tm)
    q, k, vt = _prep(fs, lf, b, s, min(512, s), tk)
    o_dil = _dilated_runs(dil, bias_mats[0], w["g_dil"], bias_mats[1], b, s, RUNS * BAND)
    o_fox = _attn(q, k, vt, w["g_fox"], "fox", nc, tq, tk).reshape(b, W_FOX, s)
    o_sb = _attn(q, k, vt, w["g_sb"], "sb", nc, tq, tk).reshape(b, W_FOX, s)
    return _outproj(h, o_dil, o_fox, o_sb, w["wo_d"], w["wo_f"], w["wo_s"], b, s, tm)


def kernel(x, mix_norm, w_in, forget_bias, head_norm, w_out, ffn_norm, dense_w_gate, dense_w_up, dense_w_down,
           router_w, moe_w_gate, moe_w_up, moe_w_down, final_norm):
    b, s, d = x.shape
    depth = mix_norm.shape[0]
    tm, nc, tq, tk = 512, 4, 512, 256
    h = x.reshape(b * s, d)
    bias_mats = (jnp.stack([_runs_bias(dd) for _, dd in DIL_PATTERNS], axis=1), _runs_perm())
    fg = final_norm.reshape(1, d)
    for layer in range(depth):
        w = _mixer_weights(w_in[layer], forget_bias[layer], head_norm[layer], w_out[layer], tq)
        h = _token_mixer(h, mix_norm[layer].reshape(1, d), w, bias_mats, b, s, tm, nc, tq, tk)
        g = ffn_norm[layer].reshape(1, d)
        i = layer // 2
        if layer % 2 == 0:
            h = _ffn(h, g, dense_w_gate[i].astype(BF16), dense_w_up[i].astype(BF16), dense_w_down[i].astype(BF16), tm)
        else:
            rw = jnp.pad(router_w[i], ((0, 0), (0, LANES - N_EXPERTS)))
            h = _moe(h, g, rw, moe_w_gate[i].astype(BF16), moe_w_up[i].astype(BF16), moe_w_down[i].astype(BF16),
                     fg, tm, final_norm=(layer == depth - 1))
    if depth % 2 == 1:
        h = _rmsnorm(h, fg, tm)
    return h.reshape(b, s, d)
```

```python
import functools

import jax
import jax.numpy as jnp
from jax import lax
from jax.experimental import pallas as pl
from jax.experimental.pallas import tpu as pltpu

D_MODEL = 1024
HEAD_DIM = 64
H_DIL, H_FOX, H_SB = 6, 5, 5
DIL_PATTERNS = ((128, 1), (512, 4), (2048, 16))
BAND = 128
D_FF = 2816
N_EXPERTS = 8
EPS = 1e-6
NEG = -1e30
LANES = 128
W_DIL = H_DIL * HEAD_DIM
W_FOX = H_FOX * HEAD_DIM
W_FS = 3 * (H_FOX + H_SB) * HEAD_DIM
VMEM_LIMIT = 56 * 1024 * 1024

F32 = jnp.float32
BF16 = jnp.bfloat16
NT = (((1,), (1,)), ((), ()))
TN = (((0,), (0,)), ((), ()))


def _dot(a, b):
    return jnp.dot(a, b, preferred_element_type=F32)


def _dot_nt(a, b):
    return lax.dot_general(a, b, NT, preferred_element_type=F32)


def _dot_tn(a, b):
    return lax.dot_general(a, b, TN, preferred_element_type=F32)


def _rms(x, g):
    return x * lax.rsqrt(jnp.mean(x * x, axis=-1, keepdims=True) + EPS) * g


def _split3(x):
    x1 = x.astype(BF16)
    r = x - x1.astype(F32)
    x2 = r.astype(BF16)
    x3 = (r - x2.astype(F32)).astype(BF16)
    return x1, x2, x3


def _cparams(sem):
    return pltpu.CompilerParams(dimension_semantics=sem, vmem_limit_bytes=VMEM_LIMIT)


def _inproj_kernel(x_ref, g_ref, wd_ref, wf_ref, wl_ref, fb_ref, dil_ref, fs_ref, lf_ref, u_ref):
    u_ref[...] = _rms(x_ref[...], g_ref[...]).astype(BF16)
    u = u_ref[...]
    for w_ref, o_ref in ((wd_ref, dil_ref), (wf_ref, fs_ref)):
        n = w_ref.shape[1]
        for c in range(0, n, 512):
            e = min(c + 512, n)
            o_ref[:, c:e] = _dot(u, w_ref[:, c:e]).astype(BF16)
    z = _dot(u, wl_ref[...]) + fb_ref[...]
    lf_ref[...] = jnp.minimum(z, 0.0) - jnp.log1p(jnp.exp(-jnp.abs(z)))


def _inproj(h, g, wd, wf, wl, fb, tm):
    t = h.shape[0]
    full = lambda a: pl.BlockSpec(a.shape, lambda i: (0, 0))
    return pl.pallas_call(
        _inproj_kernel,
        grid=(t // tm,),
        in_specs=[pl.BlockSpec((tm, D_MODEL), lambda i: (i, 0)), full(g), full(wd), full(wf), full(wl), full(fb)],
        out_specs=[pl.BlockSpec((tm, 3 * W_DIL), lambda i: (i, 0)),
                   pl.BlockSpec((tm, W_FS), lambda i: (i, 0)),
                   pl.BlockSpec((tm, LANES), lambda i: (i, 0))],
        out_shape=[jax.ShapeDtypeStruct((t, 3 * W_DIL), BF16),
                   jax.ShapeDtypeStruct((t, W_FS), BF16),
                   jax.ShapeDtypeStruct((t, LANES), F32)],
        scratch_shapes=[pltpu.VMEM((tm, D_MODEL), BF16)],
        compiler_params=_cparams(("parallel",)),
        name="inproj",
    )(h, g, wd, wf, wl, fb)


def _prep_kernel(fs_ref, lf_ref, q_ref, k_ref, vt_ref, carry_ref, *, tc, tk):
    @pl.when(pl.program_id(1) == 0)
    def _():
        carry_ref[...] = jnp.zeros_like(carry_ref)

    row = lax.broadcasted_iota(jnp.int32, (tc, tc), 0)
    col = lax.broadcasted_iota(jnp.int32, (tc, tc), 1)
    tri = jnp.where(col <= row, 1.0, 0.0).astype(BF16)
    x1, x2, x3 = _split3(lf_ref[...])
    c = _dot(tri, x1) + _dot(tri, x2) + _dot(tri, x3) + carry_ref[0:1, :]
    carry_ref[0:1, :] = c[tc - 1:tc, :]

    n_heads = H_FOX + H_SB
    for s in range(n_heads // 2):
        vt = fs_ref[:, LANES * s:LANES * (s + 1)].astype(F32).T
        for half in range(2):
            for ci in range(tc // tk):
                vt_ref[2 * s + half, ci] = vt[HEAD_DIM * half:HEAD_DIM * (half + 1), ci * tk:(ci + 1) * tk].astype(BF16)

    lane = lax.broadcasted_iota(jnp.int32, (tc, LANES), 1)
    low = lane < HEAD_DIM

    def head(col0):
        s = col0 // LANES
        x = fs_ref[:, LANES * s:LANES * (s + 1)].astype(F32)
        if (col0 // HEAD_DIM) % 2 == 1:
            x = pltpu.roll(x, HEAD_DIM, 1)
        return jnp.where(low, x, 0.0)

    base = n_heads * HEAD_DIM
    for j in range(H_FOX):
        cj = c[:, j:j + 1]
        c1 = cj.astype(BF16).astype(F32)
        r = cj - c1
        c2 = r.astype(BF16).astype(F32)
        c3 = r - c2
        qa = head(base + HEAD_DIM * j)
        ka = head(base + HEAD_DIM * (H_FOX + j))
        for off, (qv, kv) in enumerate(((c1, 1.0), (c2, 1.0), (c3, 1.0), (1.0, -c1), (1.0, -c2), (1.0, -c3))):
            sel = lane == HEAD_DIM + off
            qa = jnp.where(sel, qv, qa)
            ka = jnp.where(sel, kv, ka)
        q_ref[j] = qa.astype(BF16)
        k_ref[j] = ka.astype(BF16)
    base_sb = base + 2 * H_FOX * HEAD_DIM
    for j in range(H_SB):
        q_ref[H_FOX + j] = head(base_sb + HEAD_DIM * j).astype(BF16)
        k_ref[H_FOX + j] = head(base_sb + HEAD_DIM * (H_SB + j)).astype(BF16)


def _prep(fs, lf, b, s, tc, tk):
    nh = H_FOX + H_SB
    nk = s // tk
    return pl.pallas_call(
        functools.partial(_prep_kernel, tc=tc, tk=tk),
        grid=(b, s // tc),
        in_specs=[pl.BlockSpec((None, tc, W_FS), lambda bi, si: (bi, si, 0)),
                  pl.BlockSpec((None, tc, LANES), lambda bi, si: (bi, si, 0))],
        out_specs=[pl.BlockSpec((None, nh, tc, LANES), lambda bi, si: (bi, 0, si, 0)),
                   pl.BlockSpec((None, nh, tc, LANES), lambda bi, si: (bi, 0, si, 0)),
                   pl.BlockSpec((None, nh, tc // tk, HEAD_DIM, tk), lambda bi, si: (bi, 0, si, 0, 0))],
        out_shape=[jax.ShapeDtypeStruct((b, nh, s, LANES), BF16),
                   jax.ShapeDtypeStruct((b, nh, s, LANES), BF16),
                   jax.ShapeDtypeStruct((b, nh, nk, HEAD_DIM, tk), BF16)],
        scratch_shapes=[pltpu.VMEM((8, LANES), F32)],
        compiler_params=_cparams(("parallel", "arbitrary")),
        name="prep",
    )(fs.reshape(b, s, W_FS), lf.reshape(b, s, LANES))


LOG2E = 1.4426950408889634


def _neg_abs(x):
    return pltpu.bitcast(pltpu.bitcast(x, jnp.uint32) | jnp.uint32(0x80000000), F32)


def _attn_kernel(*refs, kind, nc, tq, tk):
    if kind == "sb":
        q_ref, k_ref, v_ref, g_ref, up_ref, o_ref, sbuf = refs
        up2 = up_ref[...]
    else:
        q_ref, k_ref, v_ref, g_ref, o_ref, sbuf = refs
    r = tq // tk
    nt = nc * r
    nfull = pl.program_id(2) * nt
    row = lax.broadcasted_iota(jnp.int32, (tk, tq), 0)
    col = lax.broadcasted_iota(jnp.int32, (tk, tq), 1)
    qs = [q_ref[c * tq:(c + 1) * tq, :] for c in range(nc)]
    every = list(range(nc))
    live_at = lambda jj: [c for c in every if jj < (c + 1) * r]

    def scores_of(j, live):
        kb = k_ref[pl.ds(pl.multiple_of(j * tk, tk), tk), :]
        return {c: _dot_nt(kb, qs[c]) for c in live}

    def park(scores, slot):
        for c, sc in scores.items():
            sbuf[slot, c] = sc

    def parked(slot):
        return {c: sbuf[slot, c] for c in every}

    def fox_update(scores, vt, carries, rels):
        out = {}
        for c, st in scores.items():
            m, l, acc = carries[c]
            if rels[c] is not None:
                st = jnp.where(row + rels[c] <= col, st, NEG)
            m_new = jnp.maximum(m, jnp.max(st, axis=0, keepdims=True))
            alpha = jnp.exp(m - m_new)
            pt = jnp.exp(st - m_new)
            out[c] = (m_new, alpha * l + jnp.sum(pt, axis=0, keepdims=True), alpha * acc + _dot(vt, pt.astype(BF16)))
        return out

    def sb_update(scores, vt, carries, rels):
        sps, laters, out = {}, {}, {}
        for c, z in scores.items():
            e = jnp.exp2(_neg_abs(z * LOG2E))
            sp = jnp.maximum(z, 0.0) + jnp.log(1.0 + e)
            if rels[c] is not None:
                sp = jnp.where(row + rels[c] < col, sp, 0.0)
            sps[c] = sp
        for c, sp in sps.items():
            hi = sp.astype(BF16)
            lo = (sp - hi.astype(F32)).astype(BF16)
            laters[c] = _dot(up2, jnp.concatenate([hi, lo], axis=0))
        for c, z in scores.items():
            rs, acc = carries[c]
            a = jnp.exp(z - sps[c] - laters[c])
            if rels[c] is not None:
                a = jnp.where(row + rels[c] < col, a, 0.0)
            out[c] = (rs + laters[c][0:1, :] + sps[c][0:1, :], acc + _dot(vt, a.astype(BF16)) * jnp.exp(-rs))
        return out

    def update(scores, j, carries, jj):
        rels = {c: None if jj is None or jj < c * r else jj * tk - c * tq for c in scores}
        new = (sb_update if kind == "sb" else fox_update)(scores, v_ref[j], carries, rels)
        return tuple(new.get(c, carries[c]) for c in every)

    zeros = jnp.zeros((HEAD_DIM, tq), F32)
    if kind == "sb":
        carries = tuple((jnp.zeros((1, tq), F32), zeros) for _ in every)
        cur = scores_of(nfull + nt - 1, live_at(nt - 1))
        for jj in reversed(range(nt)):
            if jj > 0:
                nxt = scores_of(nfull + jj - 1, live_at(jj - 1))
            else:
                park(scores_of(jnp.maximum(nfull - 1, 0), every), 0)
            carries = update(cur, nfull + jj, carries, jj)
            cur = nxt

        def pair(i2, cs):
            for half in range(2):
                j = nfull - 1 - (2 * i2 + half)
                park(scores_of(jnp.maximum(j - 1, 0), every), 1 - half)
                cs = update(parked(half), j, cs, None)
            return cs

        carries = lax.fori_loop(0, nfull // 2, pair, carries)
        outs = [acc for _, acc in carries]
    else:
        carries = tuple((jnp.full((1, tq), NEG, F32), jnp.zeros((1, tq), F32), zeros) for _ in every)
        park(scores_of(0, every), 0)

        def pair(i2, cs):
            for half in range(2):
                j = 2 * i2 + half
                park(scores_of(j + 1, every), 1 - half)
                cs = update(parked(half), j, cs, None)
            return cs

        carries = lax.fori_loop(0, nfull // 2, pair, carries)
        cur = parked(0)
        for jj in range(nt):
            if jj + 1 < nt:
                nxt = scores_of(nfull + jj + 1, live_at(jj + 1))
            carries = update(cur, nfull + jj, carries, jj)
            cur = nxt
        outs = [acc / l for _, l, acc in carries]

    g = g_ref[pl.program_id(1)]
    for c, o in enumerate(outs):
        ms = jnp.mean(o * o, axis=0, keepdims=True)
        o_ref[:, c * tq:(c + 1) * tq] = (o * lax.rsqrt(ms + EPS) * g).astype(BF16)


def _attn(q, k, vt, gains, kind, nc, tq, tk):
    b, _, s, _ = q.shape
    nk = s // tk
    tile = nc * tq
    assert (tile // tk) % 2 == 0
    base = H_FOX if kind == "sb" else 0
    in_specs = [pl.BlockSpec((None, None, tile, LANES), lambda bi, h, qi: (bi, base + h, qi, 0)),
                pl.BlockSpec((None, None, s, LANES), lambda bi, h, qi: (bi, base + h, 0, 0)),
                pl.BlockSpec((None, None, nk, HEAD_DIM, tk), lambda bi, h, qi: (bi, base + h, 0, 0, 0)),
                pl.BlockSpec(gains.shape, lambda bi, h, qi: (0, 0, 0))]
    args = [q, k, vt, gains]
    if kind == "sb":
        upper = jnp.triu(jnp.ones((tk, tk), F32), k=1).astype(BF16)
        in_specs.append(pl.BlockSpec((tk, 2 * tk), lambda bi, h, qi: (0, 0)))
        args.append(jnp.concatenate([upper, upper], axis=1))
    return pl.pallas_call(
        functools.partial(_attn_kernel, kind=kind, nc=nc, tq=tq, tk=tk),
        grid=(b, H_FOX, s // tile),
        in_specs=in_specs,
        out_specs=pl.BlockSpec((None, None, HEAD_DIM, tile), lambda bi, h, qi: (bi, h, 0, qi)),
        out_shape=jax.ShapeDtypeStruct((b, H_FOX, HEAD_DIM, s), BF16),
        scratch_shapes=[pltpu.VMEM((2, nc, tk, tq), F32)],
        compiler_params=_cparams(("parallel", "parallel", "arbitrary")),
        name=kind + "_attn",
    )(*args)


RUNS = DIL_PATTERNS[-1][1]
GROUP = RUNS * RUNS


def _runs_kernel(q_ref, k_ref, v_ref, bm_ref, g_ref, pm_ref, o_ref, qf, kf, vf, of, lf, *, tp):
    first = pl.program_id(1) == 0
    n_slab = H_DIL // 2
    span = tp // RUNS
    perm, perm_t = pm_ref[0], pm_ref[1]

    for s in range(n_slab):
        @pl.when(first)
        def _(s=s):
            kf[s] = jnp.zeros(kf.shape[1:], F32)
            vf[s] = jnp.zeros(vf.shape[1:], F32)

        @pl.when(jnp.logical_not(first))
        def _(s=s):
            for r in range(RUNS):
                kf[s, 2 * span * r:2 * span * r + span] = kf[s, 2 * span * r + span:2 * span * (r + 1)]
                vf[s, 2 * span * r:2 * span * r + span] = vf[s, 2 * span * r + span:2 * span * (r + 1)]

    for gi in range(tp // GROUP):
        rows = slice(GROUP * gi, GROUP * (gi + 1))
        for src, dst, stride, off in ((q_ref, qf, span, 0), (k_ref, kf, 2 * span, span), (v_ref, vf, 2 * span, span)):
            res = _dot(perm, src[rows, :])
            for s in range(n_slab):
                for r in range(RUNS):
                    at = stride * r + off + RUNS * gi
                    dst[s, at:at + RUNS] = res[RUNS * r:RUNS * (r + 1), LANES * s:LANES * (s + 1)]

    lane = lax.broadcasted_iota(jnp.int32, (1, LANES), 1)
    half_mask = [(lane < HEAD_DIM).astype(F32), (lane >= HEAD_DIM).astype(F32)]
    ones = jnp.ones((2 * BAND, LANES), BF16)
    heads = [(s, par) for s in range(n_slab) for par in range(2)]

    for pi, (_, d) in enumerate(DIL_PATTERNS):
        has_prev = pi > 0
        is_last = pi == len(DIL_PATTERNS) - 1
        pieces = RUNS // d
        lq = BAND // pieces

        def block(it, carry, d=d, pi=pi, has_prev=has_prev, is_last=is_last, pieces=pieces, lq=lq):
            m = it // d
            q_at = [pl.multiple_of(((it % d) + d * a) * span + lq * m, 8) for a in range(pieces)]
            k_at = [pl.multiple_of(((it % d) + d * a) * 2 * span + span - lq + lq * m, 8) for a in range(pieces)]
            gather = lambda ref, s, at, n: jnp.concatenate([ref[s, pl.ds(x, n), :] for x in at], axis=0)
            table = jnp.logical_and(first, m == 0).astype(jnp.int32)
            scores, vals = {}, {}
            for s in range(n_slab):
                qb = gather(qf, s, q_at, lq).astype(BF16)
                kcat = gather(kf, s, k_at, 2 * lq)
                vcat = gather(vf, s, k_at, 2 * lq)
                for par in range(2):
                    scores[s, par] = _dot_nt(qb, (kcat * half_mask[par]).astype(BF16)) + bm_ref[table, pi, 2 * s + par]
                    vals[s, par] = jnp.concatenate([(vcat * half_mask[par]).astype(BF16), ones], axis=1)
            o_slab = [jnp.zeros((BAND, LANES), F32) for _ in range(n_slab)]
            lses = {}
            for s, par in heads:
                h = 2 * s + par
                sc = scores[s, par]
                mx = jnp.max(sc, axis=-1, keepdims=True)
                pe = jnp.exp(sc - mx).astype(BF16)
                pv = _dot(pe, vals[s, par])
                den = pv[:, LANES:]
                o = pv[:, :LANES] / den
                lse = mx + jnp.log(den)
                if has_prev:
                    lp = jnp.concatenate([lf[h, pl.ds(x, lq), :] for x in q_at], axis=0)
                    mm = jnp.maximum(lp, lse)
                    e1 = jnp.exp(lp - mm)
                    e2 = jnp.exp(lse - mm)
                    o = (e1 * (gather(of, s, q_at, lq) * half_mask[par]) + e2 * o) / (e1 + e2)
                    lse = mm + jnp.log(e1 + e2)
                if is_last:
                    ms = jnp.sum(o * o, axis=-1, keepdims=True) * (1.0 / HEAD_DIM)
                    o = o * lax.rsqrt(ms + EPS)
                else:
                    lses[h] = lse
                o_slab[s] = o_slab[s] + o
            for a, x in enumerate(q_at):
                for s in range(n_slab):
                    of[s, pl.ds(x, lq), :] = o_slab[s][lq * a:lq * (a + 1)]
                for h, lse in lses.items():
                    lf[h, pl.ds(x, lq), :] = lse[lq * a:lq * (a + 1)]
            return carry

        lax.fori_loop(0, tp // BAND, block, 0)

    for s in range(n_slab):
        lanes = slice(LANES * s, LANES * (s + 1))
        y = (of[s] * g_ref[:, lanes]).astype(BF16)
        for gi in range(tp // GROUP):
            blk = jnp.concatenate([y[span * r + RUNS * gi:span * r + RUNS * (gi + 1)] for r in range(RUNS)], axis=0)
            o_ref[GROUP * gi:GROUP * (gi + 1), lanes] = _dot(perm_t, blk).astype(BF16)


def _runs_bias(d):
    pieces = RUNS // d
    lq = BAND // pieces
    slopes = 2.0 ** (-8.0 * jnp.arange(1, H_DIL + 1, dtype=F32) / H_DIL)
    i, j = jnp.arange(BAND), jnp.arange(2 * BAND)
    dist = (pieces * ((i % lq)[:, None] - (j % (2 * lq))[None, :] + lq) + (i // lq)[:, None] - (j // (2 * lq))[None, :])
    valid = (dist >= 0) & (dist <= BAND)
    own = (j % (2 * lq) >= lq)[None, :]
    bias = -slopes[:, None, None] * (dist * d).astype(F32)[None]
    return jnp.stack([jnp.where(valid[None], bias, NEG), jnp.where((valid & own)[None], bias, NEG)])


def _runs_perm():
    idx = jnp.arange(GROUP)
    p = (idx[None, :] == (idx[:, None] % RUNS) * RUNS + idx[:, None] // RUNS).astype(BF16)
    return jnp.stack([p, p.T])


def _dilated_runs(dil, bm, gains, perms, b, s, tp):
    assert tp == RUNS * BAND
    view = dil.reshape(b, s, 3 * W_DIL)
    part = lambda p: pl.BlockSpec((None, tp, W_DIL), lambda bi, i: (bi, i, p))
    n_slab = H_DIL // 2
    return pl.pallas_call(
        functools.partial(_runs_kernel, tp=tp),
        grid=(b, s // tp),
        in_specs=[part(0), part(1), part(2),
                  pl.BlockSpec(bm.shape, lambda bi, i: (0, 0, 0, 0, 0)), pl.BlockSpec(gains.shape, lambda bi, i: (0, 0)),
                  pl.BlockSpec(perms.shape, lambda bi, i: (0, 0, 0))],
        out_specs=pl.BlockSpec((None, tp, W_DIL), lambda bi, i: (bi, i, 0)),
        out_shape=jax.ShapeDtypeStruct((b, s, W_DIL), BF16),
        scratch_shapes=[pltpu.VMEM((n_slab, tp, LANES), F32), pltpu.VMEM((n_slab, 2 * tp, LANES), F32),
                        pltpu.VMEM((n_slab, 2 * tp, LANES), F32), pltpu.VMEM((n_slab, tp, LANES), F32),
                        pltpu.VMEM((H_DIL, tp, LANES), F32)],
        compiler_params=_cparams(("parallel", "arbitrary")),
        name="dilated",
    )(view, view, view, bm, gains, perms)


def _outproj_kernel(h_ref, od_ref, of_ref, os_ref, wd_ref, wf_ref, ws_ref, o_ref):
    o_ref[...] = (h_ref[...] + _dot(od_ref[...], wd_ref[...]) + _dot_tn(of_ref[...], wf_ref[...])
                  + _dot_tn(os_ref[...], ws_ref[...]))


def _outproj(h, od, of, os_, wd, wf, ws, b, s, tm):
    full = lambda a: pl.BlockSpec(a.shape, lambda bi, i: (0, 0))
    tile = lambda w: pl.BlockSpec((None, tm, w), lambda bi, i: (bi, i, 0))
    tile_t = pl.BlockSpec((None, W_FOX, tm), lambda bi, i: (bi, 0, i))
    return pl.pallas_call(
        _outproj_kernel,
        grid=(b, s // tm),
        in_specs=[tile(D_MODEL), tile(W_DIL), tile_t, tile_t, full(wd), full(wf), full(ws)],
        out_specs=tile(D_MODEL),
        out_shape=jax.ShapeDtypeStruct((b, s, D_MODEL), F32),
        compiler_params=_cparams(("parallel", "parallel")),
        name="outproj",
    )(h.reshape(b, s, D_MODEL), od, of, os_, wd, wf, ws).reshape(b * s, D_MODEL)


def _swiglu_tile(u, wg_ref, wu_ref, wd_ref, hid_ref):
    for c in range(0, D_FF, 256):
        g = _dot(u, wg_ref[:, c:c + 256])
        v = _dot(u, wu_ref[:, c:c + 256])
        hid_ref[:, c:c + 256] = (g * (1.0 / (1.0 + jnp.exp(-g))) * v).astype(BF16)
    return _dot(hid_ref[...], wd_ref[...])


def _ffn_kernel(x_ref, g_ref, wg_ref, wu_ref, wd_ref, o_ref, u_ref, hid_ref):
    x = x_ref[...]
    u_ref[...] = _rms(x, g_ref[...]).astype(BF16)
    o_ref[...] = x + _swiglu_tile(u_ref[...], wg_ref, wu_ref, wd_ref, hid_ref)


def _ffn(h, g, wg, wu, wd, tm):
    t = h.shape[0]
    full = lambda a: pl.BlockSpec(a.shape, lambda i: (0, 0))
    tile = pl.BlockSpec((tm, D_MODEL), lambda i: (i, 0))
    return pl.pallas_call(
        _ffn_kernel,
        grid=(t // tm,),
        in_specs=[tile, full(g), full(wg), full(wu), full(wd)],
        out_specs=tile,
        out_shape=jax.ShapeDtypeStruct((t, D_MODEL), F32),
        scratch_shapes=[pltpu.VMEM((tm, D_MODEL), BF16), pltpu.VMEM((tm, D_FF), BF16)],
        compiler_params=_cparams(("parallel",)),
        name="ffn",
    )(h, g, wg, wu, wd)


R_I1, R_I2, R_G1, R_G2, R_K1, R_K2 = 8, 9, 10, 11, 12, 13


def _router_kernel(x_ref, g_ref, rw_ref, route_ref, cnt_ref, carry_ref, *, tm):
    @pl.when(pl.program_id(0) == 0)
    def _():
        carry_ref[...] = jnp.zeros_like(carry_ref)

    lane = lax.broadcasted_iota(jnp.int32, (tm, LANES), 1)
    u = _rms(x_ref[...], g_ref[...])
    logits = jnp.dot(u, rw_ref[...], preferred_element_type=F32, precision=lax.Precision.HIGHEST)
    logits = jnp.where(lane < N_EXPERTS, logits, NEG)
    t1 = jnp.max(logits, axis=-1, keepdims=True)
    i1 = jnp.min(jnp.where(logits == t1, lane, LANES), axis=-1, keepdims=True)
    rest = jnp.where(lane == i1, NEG, logits)
    t2 = jnp.max(rest, axis=-1, keepdims=True)
    i2 = jnp.min(jnp.where(rest == t2, lane, LANES), axis=-1, keepdims=True)
    e2 = jnp.exp(t2 - t1)
    g1 = 1.0 / (1.0 + e2)
    g2 = e2 / (1.0 + e2)

    onehot = jnp.where(lane == i1, 1.0, 0.0) + jnp.where(lane == i2, 1.0, 0.0)
    row = lax.broadcasted_iota(jnp.int32, (tm, tm), 0)
    col = lax.broadcasted_iota(jnp.int32, (tm, tm), 1)
    strict = jnp.where(col < row, 1.0, 0.0).astype(BF16)
    before = _dot(strict, onehot.astype(BF16)) + carry_ref[0:1, :]
    k1 = jnp.sum(jnp.where(lane == i1, before, 0.0), axis=-1, keepdims=True)
    k2 = jnp.sum(jnp.where(lane == i2, before, 0.0), axis=-1, keepdims=True)
    carry_ref[0:1, :] = before[tm - 1:tm, :] + onehot[tm - 1:tm, :]

    rec = jnp.zeros((tm, LANES), F32)
    for ln, val in ((R_I1, i1.astype(F32)), (R_I2, i2.astype(F32)), (R_G1, g1), (R_G2, g2), (R_K1, k1), (R_K2, k2)):
        rec = jnp.where(lane == ln, val, rec)
    route_ref[...] = rec
    cnt_ref[...] = jnp.broadcast_to(carry_ref[0:1, :], cnt_ref.shape)


def _router(h, g, rw, tm):
    t = h.shape[0]
    full = lambda a: pl.BlockSpec(a.shape, lambda i: (0, 0))
    return pl.pallas_call(
        functools.partial(_router_kernel, tm=tm),
        grid=(t // tm,),
        in_specs=[pl.BlockSpec((tm, D_MODEL), lambda i: (i, 0)), full(g), full(rw)],
        out_specs=[pl.BlockSpec((tm, LANES), lambda i: (i, 0)), pl.BlockSpec((8, LANES), lambda i: (0, 0))],
        out_shape=[jax.ShapeDtypeStruct((t, LANES), F32), jax.ShapeDtypeStruct((8, LANES), F32)],
        scratch_shapes=[pltpu.VMEM((8, LANES), F32)],
        compiler_params=_cparams(("arbitrary",)),
        name="router",
    )(h, g, rw)


def _dispatch_kernel(pos_ref, gaps_ref, x_ref, g_ref, xs_ref, ubuf, zrow, sem, zsem, *, tm):
    i = pl.program_id(0)
    slot = i % 2
    ubuf[slot] = _rms(x_ref[...], g_ref[...])

    @pl.when(i == 0)
    def _():
        zrow[...] = jnp.zeros_like(zrow)
        zero_copy = lambda r: pltpu.make_async_copy(zrow.at[pl.ds(0, 1), :], xs_ref.at[pl.ds(r, 1), :], zsem)
        for e in range(gaps_ref.shape[1]):
            lo, hi = gaps_ref[0, e], gaps_ref[1, e]

            def fill(r, carry):
                zero_copy(r).start()
                return carry

            def drain(r, carry):
                zero_copy(r).wait()
                return carry

            lax.fori_loop(lo, hi, fill, 0)
            lax.fori_loop(lo, hi, drain, 0)

    def issue(r8, carry):
        base = pl.multiple_of(r8 * 8, 8)
        for j in range(8):
            for k in range(2):
                dst = pos_ref[0, k * tm + base + j]
                pltpu.make_async_copy(ubuf.at[slot, pl.ds(base + j, 1), :], xs_ref.at[pl.ds(dst, 1), :],
                                      sem.at[slot]).start()
        return carry

    lax.fori_loop(0, tm // 8, issue, 0)

    def wait_slot(s):
        for _ in range(2):
            pltpu.make_async_copy(ubuf.at[s], ubuf.at[s], sem.at[s]).wait()

    @pl.when(i > 0)
    def _():
        wait_slot(1 - slot)

    @pl.when(i == pl.num_programs(0) - 1)
    def _():
        wait_slot(slot)


def _dispatch(h, g, pos_tiles, gaps, n_rows, tm):
    t = h.shape[0]
    return pl.pallas_call(
        functools.partial(_dispatch_kernel, tm=tm),
        grid=(t // tm,),
        in_specs=[pl.BlockSpec((None, 1, 2 * tm), lambda i: (i, 0, 0), memory_space=pltpu.SMEM),
                  pl.BlockSpec(memory_space=pltpu.SMEM),
                  pl.BlockSpec((tm, D_MODEL), lambda i: (i, 0)),
                  pl.BlockSpec(g.shape, lambda i: (0, 0))],
        out_specs=pl.BlockSpec(memory_space=pl.ANY),
        out_shape=jax.ShapeDtypeStruct((n_rows, D_MODEL), F32),
        scratch_shapes=[pltpu.VMEM((2, tm, D_MODEL), F32), pltpu.VMEM((8, D_MODEL), F32),
                        pltpu.SemaphoreType.DMA((2,)), pltpu.SemaphoreType.DMA(())],
        compiler_params=_cparams(("arbitrary",)),
        name="dispatch",
    )(pos_tiles, gaps, h, g)


def _experts_kernel(te_ref, nreal_ref, x_ref, wg_ref, wu_ref, wd_ref, y_ref, hid_ref):
    del te_ref
    used = pl.program_id(0) < nreal_ref[0]

    @pl.when(used)
    def _():
        y_ref[...] = _swiglu_tile(x_ref[...].astype(BF16), wg_ref, wu_ref, wd_ref, hid_ref)

    @pl.when(jnp.logical_not(used))
    def _():
        y_ref[...] = jnp.zeros_like(y_ref)


def _experts(xs, tile_expert, n_real, wg, wu, wd, tm):
    p = xs.shape[0]
    rows = lambda i, te, nr: (jnp.minimum(i, nr[0] - 1), 0)
    expert = lambda a: pl.BlockSpec((None,) + a.shape[1:], lambda i, te, nr: (te[i], 0, 0))
    return pl.pallas_call(
        _experts_kernel,
        grid_spec=pltpu.PrefetchScalarGridSpec(
            num_scalar_prefetch=2,
            grid=(p // tm,),
            in_specs=[pl.BlockSpec((tm, D_MODEL), rows), expert(wg), expert(wu), expert(wd)],
            out_specs=pl.BlockSpec((tm, D_MODEL), lambda i, te, nr: (i, 0)),
            scratch_shapes=[pltpu.VMEM((tm, D_FF), BF16)]),
        out_shape=jax.ShapeDtypeStruct((p, D_MODEL), F32),
        compiler_params=_cparams(("arbitrary",)),
        name="experts",
    )(tile_expert, n_real, xs, wg, wu, wd)


def _combine_kernel(pos_ref, posn_ref, h_ref, route_ref, fg_ref, ys_ref, o_ref, ybuf, sem, *, tm, final_norm):
    i = pl.program_id(0)
    slot = i % 2

    def gather(p_ref, s):
        def issue(r8, carry):
            base = pl.multiple_of(r8 * 8, 8)
            for k in range(8):
                pltpu.make_async_copy(ys_ref.at[pl.ds(p_ref[0, base + k], 1), :], ybuf.at[s, pl.ds(base + k, 1), :],
                                      sem.at[s]).start()
            return carry
        lax.fori_loop(0, 2 * tm // 8, issue, 0)

    @pl.when(i == 0)
    def _():
        gather(pos_ref, slot)

    @pl.when(i + 1 < pl.num_programs(0))
    def _():
        gather(posn_ref, 1 - slot)

    pltpu.make_async_copy(ybuf.at[slot], ybuf.at[slot], sem.at[slot]).wait()
    route = route_ref[...]
    y = route[:, R_G1:R_G1 + 1] * ybuf[slot, 0:tm, :] + route[:, R_G2:R_G2 + 1] * ybuf[slot, tm:2 * tm, :]
    res = h_ref[...] + y
    if final_norm:
        res = _rms(res, fg_ref[...])
    o_ref[...] = res


def _combine(h, route, fg, pos_tiles, ys, tm, final_norm):
    t = h.shape[0]
    n = t // tm
    tile = lambda w: pl.BlockSpec((tm, w), lambda i: (i, 0))
    smem = lambda f: pl.BlockSpec((None, 1, 2 * tm), lambda i: (f(i), 0, 0), memory_space=pltpu.SMEM)
    return pl.pallas_call(
        functools.partial(_combine_kernel, tm=tm, final_norm=final_norm),
        grid=(n,),
        in_specs=[smem(lambda i: i), smem(lambda i: jnp.minimum(i + 1, n - 1)), tile(D_MODEL), tile(LANES),
                  pl.BlockSpec(fg.shape, lambda i: (0, 0)), pl.BlockSpec(memory_space=pl.ANY)],
        out_specs=tile(D_MODEL),
        out_shape=jax.ShapeDtypeStruct((t, D_MODEL), F32),
        scratch_shapes=[pltpu.VMEM((2, 2 * tm, D_MODEL), F32), pltpu.SemaphoreType.DMA((2,))],
        compiler_params=_cparams(("arbitrary",)),
        name="combine",
    )(pos_tiles, pos_tiles, h, route, fg, ys)


def _moe(h, g, rw, wg, wu, wd, fg, tm, final_norm):
    t = h.shape[0]
    n_tiles = 2 * t // tm + N_EXPERTS
    route, cnt = _router(h, g, rw, tm)
    counts = cnt[0, :N_EXPERTS].astype(jnp.int32)
    padded = (counts + tm - 1) // tm * tm
    ends = jnp.cumsum(padded)
    starts = ends - padded
    chosen = route[:, R_I1:R_I2 + 1].astype(jnp.int32)
    pos = jnp.sum(jnp.where(chosen[..., None] == jnp.arange(N_EXPERTS), starts, 0), axis=-1)
    pos = pos + route[:, R_K1:R_K2 + 1].astype(jnp.int32)
    pos_tiles = pos.reshape(t // tm, tm, 2).transpose(0, 2, 1).reshape(t // tm, 1, 2 * tm)
    n_real = ends[-1:] // tm
    tile_ids = jnp.minimum(jnp.arange(n_tiles), n_real[0] - 1)
    tile_expert = jnp.sum(tile_ids[:, None] * tm >= ends[None, :], axis=-1).astype(jnp.int32)

    gaps = jnp.stack([jnp.append(starts + counts, ends[-1]), jnp.append(ends, n_tiles * tm)]).astype(jnp.int32)
    xs = _dispatch(h, g, pos_tiles, gaps, n_tiles * tm, tm)
    ys = _experts(xs, tile_expert, n_real.astype(jnp.int32), wg, wu, wd, tm)
    return _combine(h, route, fg, pos_tiles, ys, tm, final_norm)


def _rmsnorm_kernel(x_ref, g_ref, o_ref):
    o_ref[...] = _rms(x_ref[...], g_ref[...])


def _rmsnorm(h, g, tm):
    t = h.shape[0]
    tile = pl.BlockSpec((tm, D_MODEL), lambda i: (i, 0))
    return pl.pallas_call(
        _rmsnorm_kernel, grid=(t // tm,),
        in_specs=[tile, pl.BlockSpec(g.shape, lambda i: (0, 0))], out_specs=tile,
        out_shape=jax.ShapeDtypeStruct((t, D_MODEL), F32),
        compiler_params=_cparams(("parallel",)), name="final_norm",
    )(h, g)


def _mixer_weights(w_in, forget_bias, head_norm, w_out, tq):
    d = D_MODEL
    a, b = W_DIL, W_DIL + W_FOX
    wq, wk, wv, wl = w_in[:, :d] * 0.125, w_in[:, d:2 * d], w_in[:, 2 * d:3 * d], w_in[:, 3 * d:]
    wd = jnp.concatenate([wq[:, :a], wk[:, :a], wv[:, :a]], axis=1).astype(BF16)
    wf = jnp.concatenate([wv[:, a:b], wv[:, b:], wq[:, a:b], wk[:, a:b], wq[:, b:], wk[:, b:]], axis=1).astype(BF16)
    wl = jnp.pad(wl, ((0, 0), (0, LANES - H_FOX))).astype(BF16)
    fb = jnp.pad(forget_bias, (0, LANES - H_FOX)).reshape(1, LANES)
    gcol = lambda g: jnp.broadcast_to(g.reshape(H_FOX, HEAD_DIM, 1), (H_FOX, HEAD_DIM, tq))
    return dict(wd=wd, wf=wf, wl=wl, fb=fb, wo_d=w_out[:a].astype(BF16), wo_f=w_out[a:b].astype(BF16),
                wo_s=w_out[b:].astype(BF16),
                g_dil=head_norm[:a].reshape(1, a), g_fox=gcol(head_norm[a:b]), g_sb=gcol(head_norm[b:]))


def _token_mixer(h, norm_g, w, bias_mats, b, s, tm, nc, tq, tk):
    dil, fs, lf = _inproj(h, norm_g, w["wd"], w["wf"], w["wl"], w["fb"], tm)
    q, k, vt = _prep(fs, lf, b, s, min(512, s), tk)
    o_dil = _dilated_runs(dil, bias_mats[0], w["g_dil"], bias_mats[1], b, s, RUNS * BAND)
    o_fox = _attn(q, k, vt, w["g_fox"], "fox", nc, tq, tk).reshape(b, W_FOX, s)
    o_sb = _attn(q, k, vt, w["g_sb"], "sb", nc, tq, tk).reshape(b, W_FOX, s)
    return _outproj(h, o_dil, o_fox, o_sb, w["wo_d"], w["wo_f"], w["wo_s"], b, s, tm)


def kernel(x, mix_norm, w_in, forget_bias, head_norm, w_out, ffn_norm, dense_w_gate, dense_w_up, dense_w_down,
           router_w, moe_w_gate, moe_w_up, moe_w_down, final_norm):
    b, s, d = x.shape
    depth = mix_norm.shape[0]
    tm, nc, tq, tk = 512, 4, 512, 256
    h = x.reshape(b * s, d)
    bias_mats = (jnp.stack([_runs_bias(dd) for _, dd in DIL_PATTERNS], axis=1), _runs_perm())
    fg = final_norm.reshape(1, d)
    for layer in range(depth):
        w = _mixer_weights(w_in[layer], forget_bias[layer], head_norm[layer], w_out[layer], tq)
        h = _token_mixer(h, mix_norm[layer].reshape(1, d), w, bias_mats, b, s, tm, nc, tq, tk)
        g = ffn_norm[layer].reshape(1, d)
        i = layer // 2
        if layer % 2 == 0:
            h = _ffn(h, g, dense_w_gate[i].astype(BF16), dense_w_up[i].astype(BF16), dense_w_down[i].astype(BF16), tm)
        else:
            rw = jnp.pad(router_w[i], ((0, 0), (0, LANES - N_EXPERTS)))
            h = _moe(h, g, rw, moe_w_gate[i].astype(BF16), moe_w_up[i].astype(BF16), moe_w_down[i].astype(BF16),
                     fg, tm, final_norm=(layer == depth - 1))
    if depth % 2 == 1:
        h = _rmsnorm(h, fg, tm)
    return h.reshape(b, s, d)
```

```python
import functools

import jax
import jax.numpy as jnp
from jax import lax
from jax.experimental import pallas as pl
from jax.experimental.pallas import tpu as pltpu

D_MODEL = 1024
HEAD_DIM = 64
H_DIL, H_FOX, H_SB = 6, 5, 5
DIL_PATTERNS = ((128, 1), (512, 4), (2048, 16))
BAND = 128
D_FF = 2816
N_EXPERTS = 8
EPS = 1e-6
NEG = -1e30
LANES = 128
W_DIL = H_DIL * HEAD_DIM
W_FOX = H_FOX * HEAD_DIM
W_FS = 3 * (H_FOX + H_SB) * HEAD_DIM
VMEM_LIMIT = 56 * 1024 * 1024

F32 = jnp.float32
BF16 = jnp.bfloat16
NT = (((1,), (1,)), ((), ()))
TN = (((0,), (0,)), ((), ()))


def _dot(a, b):
    return jnp.dot(a, b, preferred_element_type=F32)


def _dot_nt(a, b):
    return lax.dot_general(a, b, NT, preferred_element_type=F32)


def _dot_tn(a, b):
    return lax.dot_general(a, b, TN, preferred_element_type=F32)


def _rms(x, g):
    return x * lax.rsqrt(jnp.mean(x * x, axis=-1, keepdims=True) + EPS) * g


def _split3(x):
    x1 = x.astype(BF16)
    r = x - x1.astype(F32)
    x2 = r.astype(BF16)
    x3 = (r - x2.astype(F32)).astype(BF16)
    return x1, x2, x3


def _cparams(sem):
    return pltpu.CompilerParams(dimension_semantics=sem, vmem_limit_bytes=VMEM_LIMIT)


def _inproj_kernel(x_ref, g_ref, wd_ref, wf_ref, wl_ref, fb_ref, dil_ref, fs_ref, lf_ref, u_ref):
    u_ref[...] = _rms(x_ref[...], g_ref[...]).astype(BF16)
    u = u_ref[...]
    for w_ref, o_ref in ((wd_ref, dil_ref), (wf_ref, fs_ref)):
        n = w_ref.shape[1]
        for c in range(0, n, 512):
            e = min(c + 512, n)
            o_ref[:, c:e] = _dot(u, w_ref[:, c:e]).astype(BF16)
    z = _dot(u, wl_ref[...]) + fb_ref[...]
    lf_ref[...] = jnp.minimum(z, 0.0) - jnp.log1p(jnp.exp(-jnp.abs(z)))


def _inproj(h, g, wd, wf, wl, fb, tm):
    t = h.shape[0]
    full = lambda a: pl.BlockSpec(a.shape, lambda i: (0, 0))
    return pl.pallas_call(
        _inproj_kernel,
        grid=(t // tm,),
        in_specs=[pl.BlockSpec((tm, D_MODEL), lambda i: (i, 0)), full(g), full(wd), full(wf), full(wl), full(fb)],
        out_specs=[pl.BlockSpec((tm, 3 * W_DIL), lambda i: (i, 0)),
                   pl.BlockSpec((tm, W_FS), lambda i: (i, 0)),
                   pl.BlockSpec((tm, LANES), lambda i: (i, 0))],
        out_shape=[jax.ShapeDtypeStruct((t, 3 * W_DIL), BF16),
                   jax.ShapeDtypeStruct((t, W_FS), BF16),
                   jax.ShapeDtypeStruct((t, LANES), F32)],
        scratch_shapes=[pltpu.VMEM((tm, D_MODEL), BF16)],
        compiler_params=_cparams(("parallel",)),
        name="inproj",
    )(h, g, wd, wf, wl, fb)


def _prep_kernel(fs_ref, lf_ref, q_ref, k_ref, vt_ref, carry_ref, *, tc, tk):
    @pl.when(pl.program_id(1) == 0)
    def _():
        carry_ref[...] = jnp.zeros_like(carry_ref)

    row = lax.broadcasted_iota(jnp.int32, (tc, tc), 0)
    col = lax.broadcasted_iota(jnp.int32, (tc, tc), 1)
    tri = jnp.where(col <= row, 1.0, 0.0).astype(BF16)
    x1, x2, x3 = _split3(lf_ref[...])
    c = _dot(tri, x1) + _dot(tri, x2) + _dot(tri, x3) + carry_ref[0:1, :]
    carry_ref[0:1, :] = c[tc - 1:tc, :]

    n_heads = H_FOX + H_SB
    for s in range(n_heads // 2):
        vt = fs_ref[:, LANES * s:LANES * (s + 1)].astype(F32).T
        for half in range(2):
            for ci in range(tc // tk):
                vt_ref[2 * s + half, ci] = vt[HEAD_DIM * half:HEAD_DIM * (half + 1), ci * tk:(ci + 1) * tk].astype(BF16)

    lane = lax.broadcasted_iota(jnp.int32, (tc, LANES), 1)
    low = lane < HEAD_DIM

    def head(col0):
        s = col0 // LANES
        x = fs_ref[:, LANES * s:LANES * (s + 1)].astype(F32)
        if (col0 // HEAD_DIM) % 2 == 1:
            x = pltpu.roll(x, HEAD_DIM, 1)
        return jnp.where(low, x, 0.0)

    base = n_heads * HEAD_DIM
    for j in range(H_FOX):
        cj = c[:, j:j + 1]
        c1 = cj.astype(BF16).astype(F32)
        r = cj - c1
        c2 = r.astype(BF16).astype(F32)
        c3 = r - c2
        qa = head(base + HEAD_DIM * j)
        ka = head(base + HEAD_DIM * (H_FOX + j))
        for off, (qv, kv) in enumerate(((c1, 1.0), (c2, 1.0), (c3, 1.0), (1.0, -c1), (1.0, -c2), (1.0, -c3))):
            sel = lane == HEAD_DIM + off
            qa = jnp.where(sel, qv, qa)
            ka = jnp.where(sel, kv, ka)
        q_ref[j] = qa.astype(BF16)
        k_ref[j] = ka.astype(BF16)
    base_sb = base + 2 * H_FOX * HEAD_DIM
    for j in range(H_SB):
        q_ref[H_FOX + j] = head(base_sb + HEAD_DIM * j).astype(BF16)
        k_ref[H_FOX + j] = head(base_sb + HEAD_DIM * (H_SB + j)).astype(BF16)


def _prep(fs, lf, b, s, tc, tk):
    nh = H_FOX + H_SB
    nk = s // tk
    return pl.pallas_call(
        functools.partial(_prep_kernel, tc=tc, tk=tk),
        grid=(b, s // tc),
        in_specs=[pl.BlockSpec((None, tc, W_FS), lambda bi, si: (bi, si, 0)),
                  pl.BlockSpec((None, tc, LANES), lambda bi, si: (bi, si, 0))],
        out_specs=[pl.BlockSpec((None, nh, tc, LANES), lambda bi, si: (bi, 0, si, 0)),
                   pl.BlockSpec((None, nh, tc, LANES), lambda bi, si: (bi, 0, si, 0)),
                   pl.BlockSpec((None, nh, tc // tk, HEAD_DIM, tk), lambda bi, si: (bi, 0, si, 0, 0))],
        out_shape=[jax.ShapeDtypeStruct((b, nh, s, LANES), BF16),
                   jax.ShapeDtypeStruct((b, nh, s, LANES), BF16),
                   jax.ShapeDtypeStruct((b, nh, nk, HEAD_DIM, tk), BF16)],
        scratch_shapes=[pltpu.VMEM((8, LANES), F32)],
        compiler_params=_cparams(("parallel", "arbitrary")),
        name="prep",
    )(fs.reshape(b, s, W_FS), lf.reshape(b, s, LANES))


LOG2E = 1.4426950408889634


def _neg_abs(x):
    return pltpu.bitcast(pltpu.bitcast(x, jnp.uint32) | jnp.uint32(0x80000000), F32)


def _attn_kernel(*refs, kind, nc, tq, tk):
    if kind == "sb":
        q_ref, k_ref, v_ref, g_ref, up_ref, o_ref, sbuf = refs
        up2 = up_ref[...]
    else:
        q_ref, k_ref, v_ref, g_ref, o_ref, sbuf = refs
    r = tq // tk
    nt = nc * r
    nfull = pl.program_id(2) * nt
    row = lax.broadcasted_iota(jnp.int32, (tk, tq), 0)
    col = lax.broadcasted_iota(jnp.int32, (tk, tq), 1)
    qs = [q_ref[c * tq:(c + 1) * tq, :] for c in range(nc)]
    every = list(range(nc))
    live_at = lambda jj: [c for c in every if jj < (c + 1) * r]

    def scores_of(j, live):
        kb = k_ref[pl.ds(pl.multiple_of(j * tk, tk), tk), :]
        return {c: _dot_nt(kb, qs[c]) for c in live}

    def park(scores, slot):
        for c, sc in scores.items():
            sbuf[slot, c] = sc

    def parked(slot):
        return {c: sbuf[slot, c] for c in every}

    def fox_update(scores, vt, carries, rels):
        out = {}
        for c, st in scores.items():
            m, l, acc = carries[c]
            if rels[c] is not None:
                st = jnp.where(row + rels[c] <= col, st, NEG)
            m_new = jnp.maximum(m, jnp.max(st, axis=0, keepdims=True))
            alpha = jnp.exp(m - m_new)
            pt = jnp.exp(st - m_new)
            out[c] = (m_new, alpha * l + jnp.sum(pt, axis=0, keepdims=True), alpha * acc + _dot(vt, pt.astype(BF16)))
        return out

    def sb_update(scores, vt, carries, rels):
        sps, laters, out = {}, {}, {}
        for c, z in scores.items():
            e = jnp.exp2(_neg_abs(z * LOG2E))
            sp = jnp.maximum(z, 0.0) + jnp.log(1.0 + e)
            if rels[c] is not None:
                sp = jnp.where(row + rels[c] < col, sp, 0.0)
            sps[c] = sp
        for c, sp in sps.items():
            hi = sp.astype(BF16)
            lo = (sp - hi.astype(F32)).astype(BF16)
            laters[c] = _dot(up2, jnp.concatenate([hi, lo], axis=0))
        for c, z in scores.items():
            rs, acc = carries[c]
            a = jnp.exp(z - sps[c] - laters[c])
            if rels[c] is not None:
                a = jnp.where(row + rels[c] < col, a, 0.0)
            out[c] = (rs + laters[c][0:1, :] + sps[c][0:1, :], acc + _dot(vt, a.astype(BF16)) * jnp.exp(-rs))
        return out

    def update(scores, j, carries, jj):
        rels = {c: None if jj is None or jj < c * r else jj * tk - c * tq for c in scores}
        new = (sb_update if kind == "sb" else fox_update)(scores, v_ref[j], carries, rels)
        return tuple(new.get(c, carries[c]) for c in every)

    zeros = jnp.zeros((HEAD_DIM, tq), F32)
    if kind == "sb":
        carries = tuple((jnp.zeros((1, tq), F32), zeros) for _ in every)
        cur = scores_of(nfull + nt - 1, live_at(nt - 1))
        for jj in reversed(range(nt)):
            if jj > 0:
                nxt = scores_of(nfull + jj - 1, live_at(jj - 1))
            else:
                park(scores_of(jnp.maximum(nfull - 1, 0), every), 0)
            carries = update(cur, nfull + jj, carries, jj)
            cur = nxt

        def pair(i2, cs):
            for half in range(2):
                j = nfull - 1 - (2 * i2 + half)
                park(scores_of(jnp.maximum(j - 1, 0), every), 1 - half)
                cs = update(parked(half), j, cs, None)
            return cs

        carries = lax.fori_loop(0, nfull // 2, pair, carries)
        outs = [acc for _, acc in carries]
    else:
        carries = tuple((jnp.full((1, tq), NEG, F32), jnp.zeros((1, tq), F32), zeros) for _ in every)
        park(scores_of(0, every), 0)

        def pair(i2, cs):
            for half in range(2):
                j = 2 * i2 + half
                park(scores_of(j + 1, every), 1 - half)
                cs = update(parked(half), j, cs, None)
            return cs

        carries = lax.fori_loop(0, nfull // 2, pair, carries)
        cur = parked(0)
        for jj in range(nt):
            if jj + 1 < nt:
                nxt = scores_of(nfull + jj + 1, live_at(jj + 1))
            carries = update(cur, nfull + jj, carries, jj)
            cur = nxt
        outs = [acc / l for _, l, acc in carries]

    g = g_ref[pl.program_id(1)]
    for c, o in enumerate(outs):
        ms = jnp.mean(o * o, axis=0, keepdims=True)
        o_ref[:, c * tq:(c + 1) * tq] = (o * lax.rsqrt(ms + EPS) * g).astype(BF16)


def _attn(q, k, vt, gains, kind, nc, tq, tk):
    b, _, s, _ = q.shape
    nk = s // tk
    tile = nc * tq
    assert (tile // tk) % 2 == 0
    base = H_FOX if kind == "sb" else 0
    in_specs = [pl.BlockSpec((None, None, tile, LANES), lambda bi, h, qi: (bi, base + h, qi, 0)),
                pl.BlockSpec((None, None, s, LANES), lambda bi, h, qi: (bi, base + h, 0, 0)),
                pl.BlockSpec((None, None, nk, HEAD_DIM, tk), lambda bi, h, qi: (bi, base + h, 0, 0, 0)),
                pl.BlockSpec(gains.shape, lambda bi, h, qi: (0, 0, 0))]
    args = [q, k, vt, gains]
    if kind == "sb":
        upper = jnp.triu(jnp.ones((tk, tk), F32), k=1).astype(BF16)
        in_specs.append(pl.BlockSpec((tk, 2 * tk), lambda bi, h, qi: (0, 0)))
        args.append(jnp.concatenate([upper, upper], axis=1))
    return pl.pallas_call(
        functools.partial(_attn_kernel, kind=kind, nc=nc, tq=tq, tk=tk),
        grid=(b, H_FOX, s // tile),
        in_specs=in_specs,
        out_specs=pl.BlockSpec((None, None, HEAD_DIM, tile), lambda bi, h, qi: (bi, h, 0, qi)),
        out_shape=jax.ShapeDtypeStruct((b, H_FOX, HEAD_DIM, s), BF16),
        scratch_shapes=[pltpu.VMEM((2, nc, tk, tq), F32)],
        compiler_params=_cparams(("parallel", "parallel", "arbitrary")),
        name=kind + "_attn",
    )(*args)


RUNS = DIL_PATTERNS[-1][1]
GROUP = RUNS * RUNS


def _runs_kernel(q_ref, k_ref, v_ref, bm_ref, g_ref, pm_ref, o_ref, qf, kf, vf, of, lf, *, tp):
    first = pl.program_id(1) == 0
    n_slab = H_DIL // 2
    span = tp // RUNS
    perm, perm_t = pm_ref[0], pm_ref[1]

    for s in range(n_slab):
        @pl.when(first)
        def _(s=s):
            kf[s] = jnp.zeros(kf.shape[1:], F32)
            vf[s] = jnp.zeros(vf.shape[1:], F32)

        @pl.when(jnp.logical_not(first))
        def _(s=s):
            for r in range(RUNS):
                kf[s, 2 * span * r:2 * span * r + span] = kf[s, 2 * span * r + span:2 * span * (r + 1)]
                vf[s, 2 * span * r:2 * span * r + span] = vf[s, 2 * span * r + span:2 * span * (r + 1)]

    for gi in range(tp // GROUP):
        rows = slice(GROUP * gi, GROUP * (gi + 1))
        for src, dst, stride, off in ((q_ref, qf, span, 0), (k_ref, kf, 2 * span, span), (v_ref, vf, 2 * span, span)):
            res = _dot(perm, src[rows, :])
            for s in range(n_slab):
                for r in range(RUNS):
                    at = stride * r + off + RUNS * gi
                    dst[s, at:at + RUNS] = res[RUNS * r:RUNS * (r + 1), LANES * s:LANES * (s + 1)]

    lane = lax.broadcasted_iota(jnp.int32, (1, LANES), 1)
    half_mask = [(lane < HEAD_DIM).astype(F32), (lane >= HEAD_DIM).astype(F32)]
    ones = jnp.ones((2 * BAND, LANES), BF16)
    heads = [(s, par) for s in range(n_slab) for par in range(2)]

    for pi, (_, d) in enumerate(DIL_PATTERNS):
        has_prev = pi > 0
        is_last = pi == len(DIL_PATTERNS) - 1
        pieces = RUNS // d
        lq = BAND // pieces

        def block(it, carry, d=d, pi=pi, has_prev=has_prev, is_last=is_last, pieces=pieces, lq=lq):
            m = it // d
            q_at = [pl.multiple_of(((it % d) + d * a) * span + lq * m, 8) for a in range(pieces)]
            k_at = [pl.multiple_of(((it % d) + d * a) * 2 * span + span - lq + lq * m, 8) for a in range(pieces)]
            gather = lambda ref, s, at, n: jnp.concatenate([ref[s, pl.ds(x, n), :] for x in at], axis=0)
            table = jnp.logical_and(first, m == 0).astype(jnp.int32)
            scores, vals = {}, {}
            for s in range(n_slab):
                qb = gather(qf, s, q_at, lq).astype(BF16)
                kcat = gather(kf, s, k_at, 2 * lq)
                vcat = gather(vf, s, k_at, 2 * lq)
                for par in range(2):
                    scores[s, par] = _dot_nt(qb, (kcat * half_mask[par]).astype(BF16)) + bm_ref[table, pi, 2 * s + par]
                    vals[s, par] = jnp.concatenate([(vcat * half_mask[par]).astype(BF16), ones], axis=1)
            o_slab = [jnp.zeros((BAND, LANES), F32) for _ in range(n_slab)]
            lses = {}
            for s, par in heads:
                h = 2 * s + par
                sc = scores[s, par]
                mx = jnp.max(sc, axis=-1, keepdims=True)
                pe = jnp.exp(sc - mx).astype(BF16)
                pv = _dot(pe, vals[s, par])
                den = pv[:, LANES:]
                o = pv[:, :LANES] / den
                lse = mx + jnp.log(den)
                if has_prev:
                    lp = jnp.concatenate([lf[h, pl.ds(x, lq), :] for x in q_at], axis=0)
                    mm = jnp.maximum(lp, lse)
                    e1 = jnp.exp(lp - mm)
                    e2 = jnp.exp(lse - mm)
                    o = (e1 * (gather(of, s, q_at, lq) * half_mask[par]) + e2 * o) / (e1 + e2)
                    lse = mm + jnp.log(e1 + e2)
                if is_last:
                    ms = jnp.sum(o * o, axis=-1, keepdims=True) * (1.0 / HEAD_DIM)
                    o = o * lax.rsqrt(ms + EPS)
                else:
                    lses[h] = lse
                o_slab[s] = o_slab[s] + o
            for a, x in enumerate(q_at):
                for s in range(n_slab):
                    of[s, pl.ds(x, lq), :] = o_slab[s][lq * a:lq * (a + 1)]
                for h, lse in lses.items():
                    lf[h, pl.ds(x, lq), :] = lse[lq * a:lq * (a + 1)]
            return carry

        lax.fori_loop(0, tp // BAND, block, 0)

    for s in range(n_slab):
        lanes = slice(LANES * s, LANES * (s + 1))
        y = (of[s] * g_ref[:, lanes]).astype(BF16)
        for gi in range(tp // GROUP):
            blk = jnp.concatenate([y[span * r + RUNS * gi:span * r + RUNS * (gi + 1)] for r in range(RUNS)], axis=0)
            o_ref[GROUP * gi:GROUP * (gi + 1), lanes] = _dot(perm_t, blk).astype(BF16)


def _runs_bias(d):
    pieces = RUNS // d
    lq = BAND // pieces
    slopes = 2.0 ** (-8.0 * jnp.arange(1, H_DIL + 1, dtype=F32) / H_DIL)
    i, j = jnp.arange(BAND), jnp.arange(2 * BAND)
    dist = (pieces * ((i % lq)[:, None] - (j % (2 * lq))[None, :] + lq) + (i // lq)[:, None] - (j // (2 * lq))[None, :])
    valid = (dist >= 0) & (dist <= BAND)
    own = (j % (2 * lq) >= lq)[None, :]
    bias = -slopes[:, None, None] * (dist * d).astype(F32)[None]
    return jnp.stack([jnp.where(valid[None], bias, NEG), jnp.where((valid & own)[None], bias, NEG)])


def _runs_perm():
    idx = jnp.arange(GROUP)
    p = (idx[None, :] == (idx[:, None] % RUNS) * RUNS + idx[:, None] // RUNS).astype(BF16)
    return jnp.stack([p, p.T])


def _dilated_runs(dil, bm, gains, perms, b, s, tp):
    assert tp == RUNS * BAND
    view = dil.reshape(b, s, 3 * W_DIL)
    part = lambda p: pl.BlockSpec((None, tp, W_DIL), lambda bi, i: (bi, i, p))
    n_slab = H_DIL // 2
    return pl.pallas_call(
        functools.partial(_runs_kernel, tp=tp),
        grid=(b, s // tp),
        in_specs=[part(0), part(1), part(2),
                  pl.BlockSpec(bm.shape, lambda bi, i: (0, 0, 0, 0, 0)), pl.BlockSpec(gains.shape, lambda bi, i: (0, 0)),
                  pl.BlockSpec(perms.shape, lambda bi, i: (0, 0, 0))],
        out_specs=pl.BlockSpec((None, tp, W_DIL), lambda bi, i: (bi, i, 0)),
        out_shape=jax.ShapeDtypeStruct((b, s, W_DIL), BF16),
        scratch_shapes=[pltpu.VMEM((n_slab, tp, LANES), F32), pltpu.VMEM((n_slab, 2 * tp, LANES), F32),
                        pltpu.VMEM((n_slab, 2 * tp, LANES), F32), pltpu.VMEM((n_slab, tp, LANES), F32),
                        pltpu.VMEM((H_DIL, tp, LANES), F32)],
        compiler_params=_cparams(("parallel", "arbitrary")),
        name="dilated",
    )(view, view, view, bm, gains, perms)


def _outproj_kernel(h_ref, od_ref, of_ref, os_ref, wd_ref, wf_ref, ws_ref, o_ref):
    o_ref[...] = (h_ref[...] + _dot(od_ref[...], wd_ref[...]) + _dot_tn(of_ref[...], wf_ref[...])
                  + _dot_tn(os_ref[...], ws_ref[...]))


def _outproj(h, od, of, os_, wd, wf, ws, b, s, tm):
    full = lambda a: pl.BlockSpec(a.shape, lambda bi, i: (0, 0))
    tile = lambda w: pl.BlockSpec((None, tm, w), lambda bi, i: (bi, i, 0))
    tile_t = pl.BlockSpec((None, W_FOX, tm), lambda bi, i: (bi, 0, i))
    return pl.pallas_call(
        _outproj_kernel,
        grid=(b, s // tm),
        in_specs=[tile(D_MODEL), tile(W_DIL), tile_t, tile_t, full(wd), full(wf), full(ws)],
        out_specs=tile(D_MODEL),
        out_shape=jax.ShapeDtypeStruct((b, s, D_MODEL), F32),
        compiler_params=_cparams(("parallel", "parallel")),
        name="outproj",
    )(h.reshape(b, s, D_MODEL), od, of, os_, wd, wf, ws).reshape(b * s, D_MODEL)


def _swiglu_tile(u, wg_ref, wu_ref, wd_ref, hid_ref):
    for c in range(0, D_FF, 256):
        g = _dot(u, wg_ref[:, c:c + 256])
        v = _dot(u, wu_ref[:, c:c + 256])
        hid_ref[:, c:c + 256] = (g * (1.0 / (1.0 + jnp.exp(-g))) * v).astype(BF16)
    return _dot(hid_ref[...], wd_ref[...])


def _ffn_kernel(x_ref, g_ref, wg_ref, wu_ref, wd_ref, o_ref, u_ref, hid_ref):
    x = x_ref[...]
    u_ref[...] = _rms(x, g_ref[...]).astype(BF16)
    o_ref[...] = x + _swiglu_tile(u_ref[...], wg_ref, wu_ref, wd_ref, hid_ref)


def _ffn(h, g, wg, wu, wd, tm):
    t = h.shape[0]
    full = lambda a: pl.BlockSpec(a.shape, lambda i: (0, 0))
    tile = pl.BlockSpec((tm, D_MODEL), lambda i: (i, 0))
    return pl.pallas_call(
        _ffn_kernel,
        grid=(t // tm,),
        in_specs=[tile, full(g), full(wg), full(wu), full(wd)],
        out_specs=tile,
        out_shape=jax.ShapeDtypeStruct((t, D_MODEL), F32),
        scratch_shapes=[pltpu.VMEM((tm, D_MODEL), BF16), pltpu.VMEM((tm, D_FF), BF16)],
        compiler_params=_cparams(("parallel",)),
        name="ffn",
    )(h, g, wg, wu, wd)


R_I1, R_I2, R_G1, R_G2, R_K1, R_K2 = 8, 9, 10, 11, 12, 13


def _router_kernel(x_ref, g_ref, rw_ref, route_ref, cnt_ref, carry_ref, *, tm):
    @pl.when(pl.program_id(0) == 0)
    def _():
        carry_ref[...] = jnp.zeros_like(carry_ref)

    lane = lax.broadcasted_iota(jnp.int32, (tm, LANES), 1)
    u = _rms(x_ref[...], g_ref[...])
    logits = jnp.dot(u, rw_ref[...], preferred_element_type=F32, precision=lax.Precision.HIGHEST)
    logits = jnp.where(lane < N_EXPERTS, logits, NEG)
    t1 = jnp.max(logits, axis=-1, keepdims=True)
    i1 = jnp.min(jnp.where(logits == t1, lane, LANES), axis=-1, keepdims=True)
    rest = jnp.where(lane == i1, NEG, logits)
    t2 = jnp.max(rest, axis=-1, keepdims=True)
    i2 = jnp.min(jnp.where(rest == t2, lane, LANES), axis=-1, keepdims=True)
    e2 = jnp.exp(t2 - t1)
    g1 = 1.0 / (1.0 + e2)
    g2 = e2 / (1.0 + e2)

    onehot = jnp.where(lane == i1, 1.0, 0.0) + jnp.where(lane == i2, 1.0, 0.0)
    row = lax.broadcasted_iota(jnp.int32, (tm, tm), 0)
    col = lax.broadcasted_iota(jnp.int32, (tm, tm), 1)
    strict = jnp.where(col < row, 1.0, 0.0).astype(BF16)
    before = _dot(strict, onehot.astype(BF16)) + carry_ref[0:1, :]
    k1 = jnp.sum(jnp.where(lane == i1, before, 0.0), axis=-1, keepdims=True)
    k2 = jnp.sum(jnp.where(lane == i2, before, 0.0), axis=-1, keepdims=True)
    carry_ref[0:1, :] = before[tm - 1:tm, :] + onehot[tm - 1:tm, :]

    rec = jnp.zeros((tm, LANES), F32)
    for ln, val in ((R_I1, i1.astype(F32)), (R_I2, i2.astype(F32)), (R_G1, g1), (R_G2, g2), (R_K1, k1), (R_K2, k2)):
        rec = jnp.where(lane == ln, val, rec)
    route_ref[...] = rec
    cnt_ref[...] = jnp.broadcast_to(carry_ref[0:1, :], cnt_ref.shape)


def _router(h, g, rw, tm):
    t = h.shape[0]
    full = lambda a: pl.BlockSpec(a.shape, lambda i: (0, 0))
    return pl.pallas_call(
        functools.partial(_router_kernel, tm=tm),
        grid=(t // tm,),
        in_specs=[pl.BlockSpec((tm, D_MODEL), lambda i: (i, 0)), full(g), full(rw)],
        out_specs=[pl.BlockSpec((tm, LANES), lambda i: (i, 0)), pl.BlockSpec((8, LANES), lambda i: (0, 0))],
        out_shape=[jax.ShapeDtypeStruct((t, LANES), F32), jax.ShapeDtypeStruct((8, LANES), F32)],
        scratch_shapes=[pltpu.VMEM((8, LANES), F32)],
        compiler_params=_cparams(("arbitrary",)),
        name="router",
    )(h, g, rw)


def _dispatch_kernel(pos_ref, gaps_ref, x_ref, g_ref, xs_ref, ubuf, zrow, sem, zsem, *, tm):
    i = pl.program_id(0)
    slot = i % 2
    ubuf[slot] = _rms(x_ref[...], g_ref[...])

    @pl.when(i == 0)
    def _():
        zrow[...] = jnp.zeros_like(zrow)
        zero_copy = lambda r: pltpu.make_async_copy(zrow.at[pl.ds(0, 1), :], xs_ref.at[pl.ds(r, 1), :], zsem)
        for e in range(gaps_ref.shape[1]):
            lo, hi = gaps_ref[0, e], gaps_ref[1, e]

            def fill(r, carry):
                zero_copy(r).start()
                return carry

            def drain(r, carry):
                zero_copy(r).wait()
                return carry

            lax.fori_loop(lo, hi, fill, 0)
            lax.fori_loop(lo, hi, drain, 0)

    def issue(r8, carry):
        base = pl.multiple_of(r8 * 8, 8)
        for j in range(8):
            for k in range(2):
                dst = pos_ref[0, k * tm + base + j]
                pltpu.make_async_copy(ubuf.at[slot, pl.ds(base + j, 1), :], xs_ref.at[pl.ds(dst, 1), :],
                                      sem.at[slot]).start(priority=k)
        return carry

    lax.fori_loop(0, tm // 8, issue, 0)

    def wait_slot(s):
        for _ in range(2):
            pltpu.make_async_copy(ubuf.at[s], ubuf.at[s], sem.at[s]).wait()

    @pl.when(i > 0)
    def _():
        wait_slot(1 - slot)

    @pl.when(i == pl.num_programs(0) - 1)
    def _():
        wait_slot(slot)


def _dispatch(h, g, pos_tiles, gaps, n_rows, tm):
    t = h.shape[0]
    return pl.pallas_call(
        functools.partial(_dispatch_kernel, tm=tm),
        grid=(t // tm,),
        in_specs=[pl.BlockSpec((None, 1, 2 * tm), lambda i: (i, 0, 0), memory_space=pltpu.SMEM),
                  pl.BlockSpec(memory_space=pltpu.SMEM),
                  pl.BlockSpec((tm, D_MODEL), lambda i: (i, 0)),
                  pl.BlockSpec(g.shape, lambda i: (0, 0))],
        out_specs=pl.BlockSpec(memory_space=pl.ANY),
        out_shape=jax.ShapeDtypeStruct((n_rows, D_MODEL), F32),
        scratch_shapes=[pltpu.VMEM((2, tm, D_MODEL), F32), pltpu.VMEM((8, D_MODEL), F32),
                        pltpu.SemaphoreType.DMA((2,)), pltpu.SemaphoreType.DMA(())],
        compiler_params=_cparams(("arbitrary",)),
        name="dispatch",
    )(pos_tiles, gaps, h, g)


def _experts_kernel(te_ref, nreal_ref, x_ref, wg_ref, wu_ref, wd_ref, y_ref, hid_ref):
    del te_ref
    used = pl.program_id(0) < nreal_ref[0]

    @pl.when(used)
    def _():
        y_ref[...] = _swiglu_tile(x_ref[...].astype(BF16), wg_ref, wu_ref, wd_ref, hid_ref)

    @pl.when(jnp.logical_not(used))
    def _():
        y_ref[...] = jnp.zeros_like(y_ref)


def _experts(xs, tile_expert, n_real, wg, wu, wd, tm):
    p = xs.shape[0]
    rows = lambda i, te, nr: (jnp.minimum(i, nr[0] - 1), 0)
    expert = lambda a: pl.BlockSpec((None,) + a.shape[1:], lambda i, te, nr: (te[i], 0, 0))
    return pl.pallas_call(
        _experts_kernel,
        grid_spec=pltpu.PrefetchScalarGridSpec(
            num_scalar_prefetch=2,
            grid=(p // tm,),
            in_specs=[pl.BlockSpec((tm, D_MODEL), rows), expert(wg), expert(wu), expert(wd)],
            out_specs=pl.BlockSpec((tm, D_MODEL), lambda i, te, nr: (i, 0)),
            scratch_shapes=[pltpu.VMEM((tm, D_FF), BF16)]),
        out_shape=jax.ShapeDtypeStruct((p, D_MODEL), F32),
        compiler_params=_cparams(("arbitrary",)),
        name="experts",
    )(tile_expert, n_real, xs, wg, wu, wd)


def _combine_kernel(pos_ref, posn_ref, h_ref, route_ref, fg_ref, ys_ref, o_ref, ybuf, sem, *, tm, final_norm):
    i = pl.program_id(0)
    slot = i % 2

    def gather(p_ref, s):
        def issue(r8, carry):
            base = pl.multiple_of(r8 * 8, 8)
            for k in range(8):
                pltpu.make_async_copy(ys_ref.at[pl.ds(p_ref[0, base + k], 1), :], ybuf.at[s, pl.ds(base + k, 1), :],
                                      sem.at[s]).start(priority=k % 2)
            return carry
        lax.fori_loop(0, 2 * tm // 8, issue, 0)

    @pl.when(i == 0)
    def _():
        gather(pos_ref, slot)

    @pl.when(i + 1 < pl.num_programs(0))
    def _():
        gather(posn_ref, 1 - slot)

    pltpu.make_async_copy(ybuf.at[slot], ybuf.at[slot], sem.at[slot]).wait()
    route = route_ref[...]
    y = route[:, R_G1:R_G1 + 1] * ybuf[slot, 0:tm, :] + route[:, R_G2:R_G2 + 1] * ybuf[slot, tm:2 * tm, :]
    res = h_ref[...] + y
    if final_norm:
        res = _rms(res, fg_ref[...])
    o_ref[...] = res


def _combine(h, route, fg, pos_tiles, ys, tm, final_norm):
    t = h.shape[0]
    n = t // tm
    tile = lambda w: pl.BlockSpec((tm, w), lambda i: (i, 0))
    smem = lambda f: pl.BlockSpec((None, 1, 2 * tm), lambda i: (f(i), 0, 0), memory_space=pltpu.SMEM)
    return pl.pallas_call(
        functools.partial(_combine_kernel, tm=tm, final_norm=final_norm),
        grid=(n,),
        in_specs=[smem(lambda i: i), smem(lambda i: jnp.minimum(i + 1, n - 1)), tile(D_MODEL), tile(LANES),
                  pl.BlockSpec(fg.shape, lambda i: (0, 0)), pl.BlockSpec(memory_space=pl.ANY)],
        out_specs=tile(D_MODEL),
        out_shape=jax.ShapeDtypeStruct((t, D_MODEL), F32),
        scratch_shapes=[pltpu.VMEM((2, 2 * tm, D_MODEL), F32), pltpu.SemaphoreType.DMA((2,))],
        compiler_params=_cparams(("arbitrary",)),
        name="combine",
    )(pos_tiles, pos_tiles, h, route, fg, ys)


def _moe(h, g, rw, wg, wu, wd, fg, tm, final_norm):
    t = h.shape[0]
    n_tiles = 2 * t // tm + N_EXPERTS
    route, cnt = _router(h, g, rw, tm)
    counts = cnt[0, :N_EXPERTS].astype(jnp.int32)
    padded = (counts + tm - 1) // tm * tm
    ends = jnp.cumsum(padded)
    starts = ends - padded
    chosen = route[:, R_I1:R_I2 + 1].astype(jnp.int32)
    pos = jnp.sum(jnp.where(chosen[..., None] == jnp.arange(N_EXPERTS), starts, 0), axis=-1)
    pos = pos + route[:, R_K1:R_K2 + 1].astype(jnp.int32)
    pos_tiles = pos.reshape(t // tm, tm, 2).transpose(0, 2, 1).reshape(t // tm, 1, 2 * tm)
    n_real = ends[-1:] // tm
    tile_ids = jnp.minimum(jnp.arange(n_tiles), n_real[0] - 1)
    tile_expert = jnp.sum(tile_ids[:, None] * tm >= ends[None, :], axis=-1).astype(jnp.int32)

    gaps = jnp.stack([jnp.append(starts + counts, ends[-1]), jnp.append(ends, n_tiles * tm)]).astype(jnp.int32)
    xs = _dispatch(h, g, pos_tiles, gaps, n_tiles * tm, tm)
    ys = _experts(xs, tile_expert, n_real.astype(jnp.int32), wg, wu, wd, tm)
    return _combine(h, route, fg, pos_tiles, ys, tm, final_norm)


def _rmsnorm_kernel(x_ref, g_ref, o_ref):
    o_ref[...] = _rms(x_ref[...], g_ref[...])


def _rmsnorm(h, g, tm):
    t = h.shape[0]
    tile = pl.BlockSpec((tm, D_MODEL), lambda i: (i, 0))
    return pl.pallas_call(
        _rmsnorm_kernel, grid=(t // tm,),
        in_specs=[tile, pl.BlockSpec(g.shape, lambda i: (0, 0))], out_specs=tile,
        out_shape=jax.ShapeDtypeStruct((t, D_MODEL), F32),
        compiler_params=_cparams(("parallel",)), name="final_norm",
    )(h, g)


def _mixer_weights(w_in, forget_bias, head_norm, w_out, tq):
    d = D_MODEL
    a, b = W_DIL, W_DIL + W_FOX
    wq, wk, wv, wl = w_in[:, :d] * 0.125, w_in[:, d:2 * d], w_in[:, 2 * d:3 * d], w_in[:, 3 * d:]
    wd = jnp.concatenate([wq[:, :a], wk[:, :a], wv[:, :a]], axis=1).astype(BF16)
    wf = jnp.concatenate([wv[:, a:b], wv[:, b:], wq[:, a:b], wk[:, a:b], wq[:, b:], wk[:, b:]], axis=1).astype(BF16)
    wl = jnp.pad(wl, ((0, 0), (0, LANES - H_FOX))).astype(BF16)
    fb = jnp.pad(forget_bias, (0, LANES - H_FOX)).reshape(1, LANES)
    gcol = lambda g: jnp.broadcast_to(g.reshape(H_FOX, HEAD_DIM, 1), (H_FOX, HEAD_DIM, tq))
    return dict(wd=wd, wf=wf, wl=wl, fb=fb, wo_d=w_out[:a].astype(BF16), wo_f=w_out[a:b].astype(BF16),
                wo_s=w_out[b:].astype(BF16),
                g_dil=head_norm[:a].reshape(1, a), g_fox=gcol(head_norm[a:b]), g_sb=gcol(head_norm[b:]))


def _token_mixer(h, norm_g, w, bias_mats, b, s, tm, nc, tq, tk):
    dil, fs, lf = _inproj(h, norm_g, w["wd"], w["wf"], w["wl"], w["fb"], tm)
    q, k, vt = _prep(fs, lf, b, s, min(512, s), tk)
    o_dil = _dilated_runs(dil, bias_mats[0], w["g_dil"], bias_mats[1], b, s, RUNS * BAND)
    o_fox = _attn(q, k, vt, w["g_fox"], "fox", nc, tq, tk).reshape(b, W_FOX, s)
    o_sb = _attn(q, k, vt, w["g_sb"], "sb", nc, tq, tk).reshape(b, W_FOX, s)
    return _outproj(h, o_dil, o_fox, o_sb, w["wo_d"], w["wo_f"], w["wo_s"], b, s, tm)


def kernel(x, mix_norm, w_in, forget_bias, head_norm, w_out, ffn_norm, dense_w_gate, dense_w_up, dense_w_down,
           router_w, moe_w_gate, moe_w_up, moe_w_down, final_norm):
    b, s, d = x.shape
    depth = mix_norm.shape[0]
    tm, nc, tq, tk = 512, 4, 512, 256
    h = x.reshape(b * s, d)
    bias_mats = (jnp.stack([_runs_bias(dd) for _, dd in DIL_PATTERNS], axis=1), _runs_perm())
    fg = final_norm.reshape(1, d)
    for layer in range(depth):
        w = _mixer_weights(w_in[layer], forget_bias[layer], head_norm[layer], w_out[layer], tq)
        h = _token_mixer(h, mix_norm[layer].reshape(1, d), w, bias_mats, b, s, tm, nc, tq, tk)
        g = ffn_norm[layer].reshape(1, d)
        i = layer // 2
        if layer % 2 == 0:
            h = _ffn(h, g, dense_w_gate[i].astype(BF16), dense_w_up[i].astype(BF16), dense_w_down[i].astype(BF16), tm)
        else:
            rw = jnp.pad(router_w[i], ((0, 0), (0, LANES - N_EXPERTS)))
            h = _moe(h, g, rw, moe_w_gate[i].astype(BF16), moe_w_up[i].astype(BF16), moe_w_down[i].astype(BF16),
                     fg, tm, final_norm=(layer == depth - 1))
    if depth % 2 == 1:
        h = _rmsnorm(h, fg, tm)
    return h.reshape(b, s, d)
```
